```python
import math
import jax
import jax.numpy as jnp
from jax import lax
import numpy as np

D_MODEL = 2048
BATCH = 4
SEQ = 2048
DEPTH = 4
DEC_BATCH = 8
DEC_SEQ = 8
PAST_LEN = 16384
PAGE_SIZE = 128

N_META = 16
N_HEADS = 32
HEAD_DIM = D_MODEL // N_HEADS
Q_BLOCK = 128
GROUP_SIZE = 16
N_GROUPS = D_MODEL // GROUP_SIZE
STATE_DIM = 64
FFN_DIM = ((8 * D_MODEL // 3 + 127) // 128) * 128
N_ATTN_LAYERS = (DEPTH + 1) // 2
N_S5_LAYERS = DEPTH // 2
RMS_EPS = 1e-6

kernel_name = "fox_s5_macaron_meta_decoder_step"


def rms_norm(x, g):
    xf = x.astype(jnp.float32)
    y = xf * lax.rsqrt(jnp.mean(xf * xf, axis=-1, keepdims=True) + RMS_EPS)
    return (y * g.astype(jnp.float32)).astype(x.dtype)


def ffn_half(h, g_pre, g_post, w_up, w_down):
    u = rms_norm(h, g_pre)
    a, b = jnp.split(u @ w_up, 2, axis=-1)
    y = (jax.nn.silu(a) * b) @ w_down
    return h + 0.5 * rms_norm(y, g_post)


def fox_project(u, w_in, b_f):
    bsz, t, _ = u.shape
    q, k, v, f = jnp.split(u @ w_in, [D_MODEL, 2 * D_MODEL, 3 * D_MODEL], axis=-1)
    q = q.reshape(bsz, t, N_HEADS, HEAD_DIM)
    k = k.reshape(bsz, t, N_HEADS, HEAD_DIM)
    v = v.reshape(bsz, t, N_HEADS, HEAD_DIM)
    logf = jax.nn.log_sigmoid((f + b_f).astype(jnp.float32))
    return q, k, v, logf


def fox_attend(q, q_off, k, k_off, v, q_pos, k_pos):
    s = jnp.einsum("bqhd,bkhd->bhqk", q, k).astype(jnp.float32) * (HEAD_DIM ** -0.5)
    s = s + jnp.transpose(q_off, (0, 2, 1))[:, :, :, None] + jnp.transpose(k_off, (0, 2, 1))[:, :, None, :]
    s = jnp.where(k_pos[None, :] <= q_pos[:, None], s, -jnp.inf)
    p = jax.nn.softmax(s, axis=-1).astype(v.dtype)
    return jnp.einsum("bhqk,bkhd->bqhd", p, v)


def fox_prompt(u, w_in, b_f, w_out):
    bsz, t, _ = u.shape
    q, k, v, logf = fox_project(u, w_in, b_f)
    pos = jnp.arange(t)
    c = jnp.cumsum(logf, axis=1)
    neg_c = -c
    y_meta = fox_attend(q[:, :N_META], c[:, :N_META], k[:, :N_META], neg_c[:, :N_META],
                        v[:, :N_META], pos[:N_META], pos[:N_META])
    n_blk = (t - N_META) // Q_BLOCK
    qb = q[:, N_META:].reshape(bsz, n_blk, Q_BLOCK, N_HEADS, HEAD_DIM).transpose(1, 0, 2, 3, 4)
    cb = c[:, N_META:].reshape(bsz, n_blk, Q_BLOCK, N_HEADS).transpose(1, 0, 2, 3)
    pb = pos[N_META:].reshape(n_blk, Q_BLOCK)
    y_real = lax.map(lambda blk: fox_attend(blk[0], blk[1], k, neg_c, v, blk[2], pos), (qb, cb, pb))
    y_real = y_real.transpose(1, 0, 2, 3, 4).reshape(bsz, t - N_META, N_HEADS, HEAD_DIM)
    y = jnp.concatenate([y_meta, y_real], axis=1).reshape(bsz, t, D_MODEL) @ w_out
    return y, k, v, logf


def fox_sample(u, k_pages, v_pages, logf_pages, w_in, b_f, w_out):
    bsz, t, _ = u.shape
    q, k, v, logf = fox_project(u, w_in, b_f)
    k_past = k_pages.reshape(bsz, -1, N_HEADS, HEAD_DIM).astype(k.dtype)
    v_past = v_pages.reshape(bsz, -1, N_HEADS, HEAD_DIM).astype(v.dtype)
    lf_past = logf_pages.reshape(bsz, -1, N_HEADS).astype(jnp.float32)
    past_len = k_past.shape[1]
    r_past = lax.cumsum(lf_past, axis=1, reverse=True) - lf_past
    c_new = jnp.cumsum(logf, axis=1)
    keys = jnp.concatenate([k_past, k], axis=1)
    vals = jnp.concatenate([v_past, v], axis=1)
    k_off = jnp.concatenate([r_past, -c_new], axis=1)
    q_pos = past_len + jnp.arange(t)
    k_pos = jnp.arange(past_len + t)
    y = fox_attend(q, c_new, keys, k_off, vals, q_pos, k_pos).reshape(bsz, t, D_MODEL) @ w_out
    return y, k, v, logf


def _diag_combine(e1, e2):
    a1, b1 = e1
    a2, b2 = e2
    return (a1 * a2, a2 * b1 + b2)


def s5_mix(u, x0_re, x0_im, a_re, a_im, b_re, b_im, c_re, c_im, log_dt, d_skip, w_glu):
    f32 = jnp.float32
    bsz, t, _ = u.shape
    lam = lax.complex(a_re.astype(f32), a_im.astype(f32))
    dt = jnp.exp(log_dt.astype(f32))[:, None]
    a_bar = jnp.exp(lam * dt)
    b_bar = ((a_bar - 1.0) / lam)[..., None] * lax.complex(b_re.astype(f32), b_im.astype(f32))
    uc = u.astype(f32).reshape(bsz, t, N_GROUPS, GROUP_SIZE)
    bu = jnp.einsum("gpc,btgc->btgp", b_bar, uc)
    x0 = lax.complex(x0_re.astype(f32), x0_im.astype(f32))
    bu = bu.at[:, 0].add(a_bar * x0)
    a_full = jnp.broadcast_to(a_bar, bu.shape)
    _, states = lax.associative_scan(_diag_combine, (a_full, bu), axis=1)
    c_mat = lax.complex(c_re.astype(f32), c_im.astype(f32))
    y = jnp.real(jnp.einsum("gcp,btgp->btgc", c_mat, states)).reshape(bsz, t, D_MODEL)
    y = y + d_skip.astype(f32) * u.astype(f32)
    y = jax.nn.gelu(y).astype(u.dtype)
    a, b = jnp.split(y @ w_glu, 2, axis=-1)
    final = states[:, -1]
    return a * jax.nn.sigmoid(b), jnp.real(final), jnp.imag(final)


def setup_inputs(seed: int = 0) -> dict:
    key = jax.random.key(seed)
    ks = jax.random.split(key, 24)
    f32 = jnp.float32
    n_pages = PAST_LEN // PAGE_SIZE
    n_used = DEC_BATCH * n_pages
    n_pool = n_used + max(1, n_used // 4)

    def nrm(k, shape, scale):
        return jax.random.normal(k, shape, f32) * scale

    x_prompt = nrm(ks[0], (BATCH, SEQ, D_MODEL), 1.0)
    x_sample = nrm(ks[1], (DEC_BATCH, DEC_SEQ, D_MODEL), 1.0)
    cache_k = nrm(ks[2], (N_ATTN_LAYERS, n_pool, PAGE_SIZE, N_HEADS, HEAD_DIM), 1.0)
    cache_v = nrm(ks[3], (N_ATTN_LAYERS, n_pool, PAGE_SIZE, N_HEADS, HEAD_DIM), 1.0)
    cache_logf = jax.nn.log_sigmoid(3.0 + nrm(ks[4], (N_ATTN_LAYERS, n_pool, PAGE_SIZE, N_HEADS), 0.5))
    state_s5_re = nrm(ks[5], (N_S5_LAYERS, DEC_BATCH, N_GROUPS, STATE_DIM), 0.1)
    state_s5_im = nrm(ks[6], (N_S5_LAYERS, DEC_BATCH, N_GROUPS, STATE_DIM), 0.1)
    page_table = jax.random.permutation(ks[7], n_pool)[:n_used].reshape(DEC_BATCH, n_pages).astype(jnp.int32)
    meta_tokens = nrm(ks[8], (N_META, D_MODEL), 1.0)
    norm_g = 1.0 + nrm(ks[9], (DEPTH, 6, D_MODEL), 0.02)
    ffn_w_up = nrm(ks[10], (DEPTH, 2, D_MODEL, 2 * FFN_DIM), D_MODEL ** -0.5)
    ffn_w_down = nrm(ks[11], (DEPTH, 2, FFN_DIM, D_MODEL), FFN_DIM ** -0.5)
    attn_w_in = nrm(ks[12], (N_ATTN_LAYERS, D_MODEL, 3 * D_MODEL + N_HEADS), D_MODEL ** -0.5)
    attn_b_f = jax.random.uniform(ks[13], (N_ATTN_LAYERS, N_HEADS), f32, 2.0, 5.0)
    attn_w_out = nrm(ks[14], (N_ATTN_LAYERS, D_MODEL, D_MODEL), D_MODEL ** -0.5)
    n_idx = jnp.arange(STATE_DIM, dtype=f32)
    s5_A_re = -0.5 + nrm(ks[15], (N_S5_LAYERS, N_GROUPS, STATE_DIM), 0.01)
    s5_A_im = jnp.pi * n_idx + nrm(ks[16], (N_S5_LAYERS, N_GROUPS, STATE_DIM), 0.01)
    s5_B_re = nrm(ks[17], (N_S5_LAYERS, N_GROUPS, STATE_DIM, GROUP_SIZE), (2 * GROUP_SIZE) ** -0.5)
    s5_B_im = nrm(ks[18], (N_S5_LAYERS, N_GROUPS, STATE_DIM, GROUP_SIZE), (2 * GROUP_SIZE) ** -0.5)
    s5_C_re = nrm(ks[19], (N_S5_LAYERS, N_GROUPS, GROUP_SIZE, STATE_DIM), STATE_DIM ** -0.5)
    s5_C_im = nrm(ks[20], (N_S5_LAYERS, N_GROUPS, GROUP_SIZE, STATE_DIM), STATE_DIM ** -0.5)
    s5_log_dt = jax.random.uniform(ks[21], (N_S5_LAYERS, N_GROUPS), f32, math.log(0.001), math.log(0.1))
    s5_D = nrm(ks[22], (N_S5_LAYERS, D_MODEL), 1.0)
    s5_w_glu = nrm(ks[23], (N_S5_LAYERS, D_MODEL, 2 * D_MODEL), D_MODEL ** -0.5)
    return {"x_prompt": x_prompt, "x_sample": x_sample, "cache_k": cache_k, "cache_v": cache_v,
            "cache_logf": cache_logf, "state_s5_re": state_s5_re, "state_s5_im": state_s5_im,
            "page_table": page_table, "meta_tokens": meta_tokens, "norm_g": norm_g,
            "ffn_w_up": ffn_w_up, "ffn_w_down": ffn_w_down, "attn_w_in": attn_w_in,
            "attn_b_f": attn_b_f, "attn_w_out": attn_w_out, "s5_A_re": s5_A_re, "s5_A_im": s5_A_im,
            "s5_B_re": s5_B_re, "s5_B_im": s5_B_im, "s5_C_re": s5_C_re, "s5_C_im": s5_C_im,
            "s5_log_dt": s5_log_dt, "s5_D": s5_D, "s5_w_glu": s5_w_glu}


def reference(x_prompt, x_sample, cache_k, cache_v, cache_logf, state_s5_re, state_s5_im, page_table,
              meta_tokens, norm_g, ffn_w_up, ffn_w_down, attn_w_in, attn_b_f, attn_w_out,
              s5_A_re, s5_A_im, s5_B_re, s5_B_im, s5_C_re, s5_C_im, s5_log_dt, s5_D, s5_w_glu):
    f32 = jnp.float32
    bsz = x_prompt.shape[0]
    meta = jnp.broadcast_to(meta_tokens[None].astype(x_prompt.dtype), (bsz, N_META, D_MODEL))
    hp = jnp.concatenate([meta, x_prompt], axis=1)
    hs = x_sample
    k_pr, v_pr, lf_pr, sre_pr, sim_pr = [], [], [], [], []
    k_sa, v_sa, lf_sa, sre_sa, sim_sa = [], [], [], [], []
    for i in range(DEPTH):
        g = norm_g[i]
        hp = ffn_half(hp, g[0], g[1], ffn_w_up[i, 0], ffn_w_down[i, 0])
        hs = ffn_half(hs, g[0], g[1], ffn_w_up[i, 0], ffn_w_down[i, 0])
        up = rms_norm(hp, g[2])
        us = rms_norm(hs, g[2])
        if i % 2 == 0:
            li = i // 2
            yp, kp_, vp_, lfp_ = fox_prompt(up, attn_w_in[li], attn_b_f[li], attn_w_out[li])
            ys, ks_, vs_, lfs_ = fox_sample(us, cache_k[li, page_table], cache_v[li, page_table],
                                            cache_logf[li, page_table], attn_w_in[li], attn_b_f[li],
                                            attn_w_out[li])
            k_pr.append(kp_); v_pr.append(vp_); lf_pr.append(lfp_)
            k_sa.append(ks_); v_sa.append(vs_); lf_sa.append(lfs_)
        else:
            si = i // 2
            prm = (s5_A_re[si], s5_A_im[si], s5_B_re[si], s5_B_im[si], s5_C_re[si], s5_C_im[si],
                   s5_log_dt[si], s5_D[si], s5_w_glu[si])
            zeros = jnp.zeros((bsz, N_GROUPS, STATE_DIM), f32)
            yp, re_p, im_p = s5_mix(up, zeros, zeros, *prm)
            ys, re_s, im_s = s5_mix(us, state_s5_re[si], state_s5_im[si], *prm)
            sre_pr.append(re_p); sim_pr.append(im_p)
            sre_sa.append(re_s); sim_sa.append(im_s)
        hp = hp + rms_norm(yp, g[3])
        hs = hs + rms_norm(ys, g[3])
        hp = ffn_half(hp, g[4], g[5], ffn_w_up[i, 1], ffn_w_down[i, 1])
        hs = ffn_half(hs, g[4], g[5], ffn_w_up[i, 1], ffn_w_down[i, 1])
    return (hp[:, N_META:], hs,
            jnp.stack(k_pr), jnp.stack(v_pr), jnp.stack(lf_pr), jnp.stack(sre_pr), jnp.stack(sim_pr),
            jnp.stack(k_sa), jnp.stack(v_sa), jnp.stack(lf_sa), jnp.stack(sre_sa), jnp.stack(sim_sa))
```

```python
import functools

import jax
import jax.numpy as jnp
from jax import lax
from jax.experimental import pallas as pl
from jax.experimental.pallas import tpu as pltpu

F32 = jnp.float32
BF16 = jnp.bfloat16
RMS_EPS = 1e-6
LANES = 128
MIB = 1024 * 1024
NT_DIMS = (((1,), (1,)), ((), ()))


def _params(vmem_mib, semantics=None):
    return pltpu.CompilerParams(dimension_semantics=semantics, vmem_limit_bytes=vmem_mib * MIB)


def _rms(x, g):
    ms = jnp.mean(x * x, axis=-1, keepdims=True)
    return (x * lax.rsqrt(ms + RMS_EPS)) * g


def _split3(x):
    hi = x.astype(BF16)
    r1 = x - hi.astype(F32)
    mid = r1.astype(BF16)
    lo = (r1 - mid.astype(F32)).astype(BF16)
    return hi, mid, lo


def _dot(a, b):
    return jnp.dot(a, b, preferred_element_type=F32)


def _dot_nt(a, b):
    return lax.dot_general(a, b, NT_DIMS, preferred_element_type=F32)


def _rmsnorm_kernel(h_ref, g_ref, o_ref):
    o_ref[...] = _rms(h_ref[...], g_ref[...]).astype(o_ref.dtype)


def _rmsnorm(h, g, tm):
    m, d = h.shape
    return pl.pallas_call(
        _rmsnorm_kernel,
        grid=(m // tm,),
        in_specs=[pl.BlockSpec((tm, d), lambda i: (i, 0)), pl.BlockSpec((1, d), lambda i: (0, 0))],
        out_specs=pl.BlockSpec((tm, d), lambda i: (i, 0)),
        out_shape=jax.ShapeDtypeStruct((m, d), BF16),
        compiler_params=_params(40),
        name="rmsnorm",
    )(h, g.reshape(1, d))


def _ffn_up_kernel(u_ref, wa_ref, wb_ref, o_ref, w_scr, *, n_chunks, ch):
    w_scr[:, :LANES] = wa_ref[...].astype(BF16)
    w_scr[:, LANES:] = wb_ref[...].astype(BF16)

    def body(c, carry):
        rows = pl.ds(pl.multiple_of(c * ch, ch), ch)
        z = _dot(u_ref[rows, :], w_scr[...])
        a = z[:, :LANES]
        b = z[:, LANES:]
        o_ref[rows, :] = (a * jax.nn.sigmoid(a) * b).astype(o_ref.dtype)
        return carry

    lax.fori_loop(0, n_chunks, body, 0)


def _ffn_up(u, w_up, layer, half, ch):
    m, d = u.shape
    f = w_up.shape[-1] // 2
    nt = f // LANES
    kern = functools.partial(_ffn_up_kernel, n_chunks=m // ch, ch=ch)
    return pl.pallas_call(
        kern,
        grid=(nt,),
        in_specs=[
            pl.BlockSpec(memory_space=pltpu.VMEM),
            pl.BlockSpec((None, None, d, LANES), lambda j: (layer, half, 0, j)),
            pl.BlockSpec((None, None, d, LANES), lambda j: (layer, half, 0, j + nt)),
        ],
        out_specs=pl.BlockSpec((m, LANES), lambda j: (0, j)),
        out_shape=jax.ShapeDtypeStruct((m, f), BF16),
        scratch_shapes=[pltpu.VMEM((d, 2 * LANES), BF16)],
        compiler_params=_params(56),
        name="ffn_up",
    )(u, w_up, w_up)


def _residual_epilogue(y, h_ref, gp_ref, gn_ref, hn_ref, un_ref, scale):
    hn = h_ref[...] + scale * _rms(y, gp_ref[...])
    hn_ref[...] = hn
    if un_ref is not None:
        un_ref[...] = _rms(hn, gn_ref[...]).astype(un_ref.dtype)


def _down_tiling(f):
    nt = f // LANES
    for dmul in (8, 6, 4, 2, 7, 5, 3):
        if nt % dmul == 0 and nt // dmul >= 1:
            return dmul * LANES, nt // dmul, False
    for dmul in (8, 6, 4, 2, 7, 5, 3):
        if (nt - 1) % dmul == 0:
            return dmul * LANES, (nt - 1) // dmul, True
    return LANES, nt, False


def _ffn_down_kernel(*refs, nk, scale, has_tail, has_next):
    refs = list(refs)
    xm_ref = refs.pop(0)
    xt_ref = refs.pop(0) if has_tail else None
    wm_ref = refs.pop(0)
    wt_ref = refs.pop(0) if has_tail else None
    h_ref = refs.pop(0)
    gp_ref = refs.pop(0)
    gn_ref = refs.pop(0) if has_next else None
    hn_ref = refs.pop(0)
    un_ref = refs.pop(0) if has_next else None
    acc_ref = refs.pop(0)
    k = pl.program_id(1)
    part = _dot(xm_ref[...], wm_ref[...])

    @pl.when(k == 0)
    def _():
        if has_tail:
            acc_ref[...] = part + _dot(xt_ref[...], wt_ref[...])
        else:
            acc_ref[...] = part

    @pl.when(k > 0)
    def _():
        acc_ref[...] += part

    @pl.when(k == nk - 1)
    def _():
        _residual_epilogue(acc_ref[...], h_ref, gp_ref, gn_ref, hn_ref, un_ref, scale)


def _ffn_down(act, w_down, layer, half, h, g_post, g_next, u_dtype, tm):
    m, f = act.shape
    d = h.shape[1]
    tk, nk, has_tail = _down_tiling(f)
    has_next = g_next is not None
    tail_blk = (f - LANES) // LANES
    in_specs = [pl.BlockSpec((tm, tk), lambda i, k: (i, k))]
    args = [act]
    if has_tail:
        in_specs.append(pl.BlockSpec((tm, LANES), lambda i, k: (i, tail_blk)))
        args.append(act)
    in_specs.append(pl.BlockSpec((None, None, tk, d), lambda i, k: (layer, half, k, 0)))
    args.append(w_down)
    if has_tail:
        in_specs.append(pl.BlockSpec((None, None, LANES, d), lambda i, k: (layer, half, tail_blk, 0)))
        args.append(w_down)
    row_spec = pl.BlockSpec((tm, d), lambda i, k: (i, 0))
    vec_spec = pl.BlockSpec((1, d), lambda i, k: (0, 0))
    in_specs += [row_spec, vec_spec]
    args += [h, g_post.reshape(1, d)]
    out_specs = [row_spec]
    out_shape = [jax.ShapeDtypeStruct((m, d), F32)]
    if has_next:
        in_specs.append(vec_spec)
        args.append(g_next.reshape(1, d))
        out_specs.append(row_spec)
        out_shape.append(jax.ShapeDtypeStruct((m, d), u_dtype))
    kern = functools.partial(_ffn_down_kernel, nk=nk, scale=0.5, has_tail=has_tail, has_next=has_next)
    res = pl.pallas_call(
        kern,
        grid=(m // tm, nk),
        in_specs=in_specs,
        out_specs=out_specs,
        out_shape=out_shape,
        scratch_shapes=[pltpu.VMEM((tm, d), F32)],
        compiler_params=_params(56, ("parallel", "arbitrary")),
        name="ffn_down",
    )(*args)
    return (res[0], res[1]) if has_next else (res[0], None)


def _proj_res_kernel(x_ref, w_ref, h_ref, gp_ref, gn_ref, hn_ref, un_ref, acc_ref, *, nn, tn):
    j = pl.program_id(1)
    col = pl.multiple_of(j * tn, tn)
    acc_ref[:, pl.ds(col, tn)] = _dot(x_ref[...], w_ref[...])

    @pl.when(j == nn - 1)
    def _():
        _residual_epilogue(acc_ref[...], h_ref, gp_ref, gn_ref, hn_ref, un_ref, 1.0)


def _glu_res_kernel(x_ref, wa_ref, wb_ref, h_ref, gp_ref, gn_ref, hn_ref, un_ref, acc_ref, *, nn, tn):
    j = pl.program_id(1)
    col = pl.multiple_of(j * tn, tn)
    x = x_ref[...]
    a = _dot(x, wa_ref[...])
    b = _dot(x, wb_ref[...])
    acc_ref[:, pl.ds(col, tn)] = a * jax.nn.sigmoid(b)

    @pl.when(j == nn - 1)
    def _():
        _residual_epilogue(acc_ref[...], h_ref, gp_ref, gn_ref, hn_ref, un_ref, 1.0)


def _mixer_out(x, w, layer, glu, h, g_post, g_next, tm, tn):
    m, kdim = x.shape
    d = h.shape[1]
    nn = d // tn
    row_spec = pl.BlockSpec((tm, d), lambda i, j: (i, 0))
    vec_spec = pl.BlockSpec((1, d), lambda i, j: (0, 0))
    in_specs = [pl.BlockSpec((tm, kdim), lambda i, j: (i, 0)),
                pl.BlockSpec((None, kdim, tn), lambda i, j: (layer, 0, j))]
    args = [x, w]
    if glu:
        in_specs.append(pl.BlockSpec((None, kdim, tn), lambda i, j: (layer, 0, j + nn)))
        args.append(w)
    in_specs += [row_spec, vec_spec, vec_spec]
    args += [h, g_post.reshape(1, d), g_next.reshape(1, d)]
    body = _glu_res_kernel if glu else _proj_res_kernel
    return pl.pallas_call(
        functools.partial(body, nn=nn, tn=tn),
        grid=(m // tm, nn),
        in_specs=in_specs,
        out_specs=[row_spec, row_spec],
        out_shape=[jax.ShapeDtypeStruct((m, d), F32), jax.ShapeDtypeStruct((m, d), BF16)],
        scratch_shapes=[pltpu.VMEM((tm, d), F32)],
        compiler_params=_params(56, ("parallel", "arbitrary")),
        name="mixer_out_glu" if glu else "mixer_out",
    )(*args)


def _log_sigmoid(x):
    return jnp.minimum(x, 0.0) - jnp.log1p(jnp.exp(-jnp.abs(x)))


def _qkv_kernel(x_ref, w_ref, wf_ref, bf_ref, o_ref, lf_ref, *, n_heads):
    j = pl.program_id(1)
    x = x_ref[...]
    o_ref[...] = _dot(x, w_ref[...])

    @pl.when(j == 0)
    def _():
        f = _dot(x, wf_ref[...])[:, :n_heads] + bf_ref[...]
        lf_ref[...] = _log_sigmoid(f)


def _qkv_proj(u, w_in, b_f, layer, tm, tn):
    m, d = u.shape
    n_heads = b_f.shape[-1]
    n_main = 3 * d
    return pl.pallas_call(
        functools.partial(_qkv_kernel, n_heads=n_heads),
        grid=(m // tm, n_main // tn),
        in_specs=[
            pl.BlockSpec((tm, d), lambda i, j: (i, 0)),
            pl.BlockSpec((None, d, tn), lambda i, j: (layer, 0, j)),
            pl.BlockSpec((None, d, LANES), lambda i, j: (layer, 0, n_main // LANES)),
            pl.BlockSpec((None, 1, n_heads), lambda i, j: (layer, 0, 0)),
        ],
        out_specs=[pl.BlockSpec((tm, tn), lambda i, j: (i, j)),
                   pl.BlockSpec((tm, n_heads), lambda i, j: (i, 0))],
        out_shape=[jax.ShapeDtypeStruct((m, n_main), F32), jax.ShapeDtypeStruct((m, n_heads), F32)],
        compiler_params=_params(48, ("parallel", "arbitrary")),
        name="qkv_proj",
    )(u, w_in, w_in, b_f.reshape(b_f.shape[0], 1, n_heads))


def _cumsum_kernel(lf_ref, o_ref, *, t_len, chunk, n_heads):
    r = lax.broadcasted_iota(jnp.int32, (chunk, chunk), 0)
    c = lax.broadcasted_iota(jnp.int32, (chunk, chunk), 1)
    tri = jnp.where(c <= r, 1.0, 0.0).astype(BF16)
    carry = jnp.zeros((1, n_heads), F32)
    for s in range(t_len // chunk):
        x = lf_ref[s * chunk:(s + 1) * chunk, :]
        hi, mid, lo = _split3(x)
        cs = _dot(tri, hi) + _dot(tri, mid) + _dot(tri, lo) + carry
        carry = cs[chunk - 1:chunk, :]
        chi, cmid, clo = _split3(cs)
        o_ref[s * chunk:(s + 1) * chunk, 0:n_heads] = chi
        o_ref[s * chunk:(s + 1) * chunk, n_heads:2 * n_heads] = cmid
        o_ref[s * chunk:(s + 1) * chunk, 2 * n_heads:3 * n_heads] = clo


def _cumsum_chunk(t_len):
    for c in (688, 256, 128, 64, 48, 16):
        if t_len % c == 0:
            return c
    return t_len


def _prompt_cumsum(logf, batch, t_len):
    n_heads = logf.shape[1]
    kern = functools.partial(_cumsum_kernel, t_len=t_len, chunk=_cumsum_chunk(t_len), n_heads=n_heads)
    return pl.pallas_call(
        kern,
        grid=(batch,),
        in_specs=[pl.BlockSpec((t_len, n_heads), lambda b: (b, 0))],
        out_specs=pl.BlockSpec((t_len, 3 * n_heads), lambda b: (b, 0)),
        out_shape=jax.ShapeDtypeStruct((batch * t_len, 3 * n_heads), BF16),
        compiler_params=_params(32, ("parallel",)),
        name="prompt_cumsum",
    )(logf)


def _attn_prompt_kernel(q_ref, k_ref, v_ref, cp_ref, o_ref, qa_scr, ka_scr, v_scr,
                        *, t_len, n_meta, tq, n_heads, head_dim):
    hp = pl.program_id(1)
    half = LANES // 2
    lane = lax.broadcasted_iota(jnp.int32, (1, LANES), 1)
    r3 = lax.broadcasted_iota(jnp.int32, (3 * n_heads, LANES), 0)
    l3 = lax.broadcasted_iota(jnp.int32, (3 * n_heads, LANES), 1)
    cp = cp_ref[...]
    q = (q_ref[...] * (head_dim ** -0.5)).astype(BF16)
    k = k_ref[...].astype(BF16)
    v_scr[...] = v_ref[...].astype(BF16)
    for hh in range(2):
        head = 2 * hp + hh
        base = half * (1 - hh)
        sel_q = jnp.zeros((3 * n_heads, LANES), F32)
        sel_k = jnp.zeros((3 * n_heads, LANES), F32)
        for part in range(3):
            row_hit = r3 == head + part * n_heads
            sel_q = sel_q + jnp.where(row_hit & (l3 == base + part), 1.0, 0.0)
            sel_k = sel_k + jnp.where(row_hit & (l3 == base + 3 + part), -1.0, 0.0)
        ones_q = jnp.where((lane >= base + 3) & (lane < base + 6), 1.0, 0.0)
        ones_k = jnp.where((lane >= base) & (lane < base + 3), 1.0, 0.0)
        ex_q = (_dot(cp, sel_q.astype(BF16)) + ones_q).astype(BF16)
        ex_k = (_dot(cp, sel_k.astype(BF16)) + ones_k).astype(BF16)
        data = (lane >= half * hh) & (lane < half * hh + half)
        qa_scr[hh] = jnp.where(data, q, ex_q)
        ka_scr[hh] = jnp.where(data, k, ex_k)

    tiles = [(0, n_meta)] + [(n_meta + i * tq, tq) for i in range((t_len - n_meta) // tq)]
    for q0, tl in tiles:
        row = lax.broadcasted_iota(jnp.int32, (tl, tl), 0)
        col = lax.broadcasted_iota(jnp.int32, (tl, tl), 1)
        causal = col <= row
        outs = []
        for hh in range(2):
            qt = qa_scr[hh, q0:q0 + tl, :]
            sd = jnp.where(causal, _dot_nt(qt, ka_scr[hh, q0:q0 + tl, :]), -jnp.inf)
            m = jnp.max(sd, axis=-1, keepdims=True)
            if q0 > 0:
                so = _dot_nt(qt, ka_scr[hh, 0:q0, :])
                m = jnp.maximum(m, jnp.max(so, axis=-1, keepdims=True))
            pd = jnp.exp(sd - m)
            den = jnp.sum(pd, axis=-1, keepdims=True)
            o = _dot(pd.astype(BF16), v_scr[q0:q0 + tl, :])
            if q0 > 0:
                po = jnp.exp(so - m)
                den = den + jnp.sum(po, axis=-1, keepdims=True)
                o = o + _dot(po.astype(BF16), v_scr[0:q0, :])
            outs.append(o / den)
        o_ref[q0:q0 + tl, :] = jnp.where(lane < half, outs[0], outs[1]).astype(o_ref.dtype)


def _attn_prompt(qkv, cparts, batch, t_len, n_meta, n_heads, tq):
    d = qkv.shape[1] // 3
    head_dim = d // n_heads
    assert 2 * head_dim == LANES, "head pairs must fill one lane tile"
    assert (t_len - n_meta) % tq == 0
    npair = n_heads // 2
    kern = functools.partial(_attn_prompt_kernel, t_len=t_len, n_meta=n_meta, tq=tq,
                             n_heads=n_heads, head_dim=head_dim)
    blk = lambda off: pl.BlockSpec((t_len, LANES), lambda b, p: (b, off + p))
    return pl.pallas_call(
        kern,
        grid=(batch, npair),
        in_specs=[blk(0), blk(npair), blk(2 * npair),
                  pl.BlockSpec((t_len, 3 * n_heads), lambda b, p: (b, 0))],
        out_specs=pl.BlockSpec((t_len, LANES), lambda b, p: (b, p)),
        out_shape=jax.ShapeDtypeStruct((batch * t_len, d), BF16),
        scratch_shapes=[pltpu.VMEM((2, t_len, LANES), BF16), pltpu.VMEM((2, t_len, LANES), BF16),
                        pltpu.VMEM((t_len, LANES), BF16)],
        compiler_params=_params(48, ("parallel", "arbitrary")),
        name="attn_prompt",
    )(qkv, qkv, qkv, cparts)


def _attn_decode_kernel(pt_ref, q_ref, kn_ref, vn_ref, lfn_ref, *rest,
                        n_pg, n_heads, head_dim, n_new, page, n_steps):
    del pt_ref
    k_refs = rest[:n_pg]
    v_refs = rest[n_pg:2 * n_pg]
    lf_refs = rest[2 * n_pg:3 * n_pg]
    o_ref = rest[3 * n_pg]
    q_scr, s_scr, p_scr, acc_scr, m_scr, l_scr, tail_scr, cnew_scr, alpha_scr = rest[3 * n_pg + 1:]
    step = pl.program_id(1)
    rows = n_heads * n_new
    er = lax.broadcasted_iota(jnp.int32, (rows, n_heads), 0)
    ec = lax.broadcasted_iota(jnp.int32, (rows, n_heads), 1)
    expand = jnp.where((er >= ec * n_new) & (er < (ec + 1) * n_new), 1.0, 0.0).astype(BF16)

    def spread(lf):
        return [_dot_nt(expand, part).astype(BF16) for part in _split3(lf)]

    def head_update(h8, p_h, v_h, first):
        contrib = _dot(p_h, v_h)
        if first:
            acc_scr[h8, :] = contrib
        else:
            acc_scr[h8, :] = alpha_scr[h8, 0:head_dim] * acc_scr[h8, :] + contrib

    @pl.when(step == 0)
    def _():
        scale = head_dim ** -0.5
        for h in range(n_heads):
            q_scr[h * n_new:(h + 1) * n_new, :] = q_ref[:, h * head_dim:(h + 1) * head_dim] * scale
        kr = lax.broadcasted_iota(jnp.int32, (n_new, LANES), 0)
        kc = lax.broadcasted_iota(jnp.int32, (n_new, LANES), 1)
        upper = jnp.where(kr <= kc, 1.0, 0.0).astype(BF16)
        parts = spread(lfn_ref[...])
        cum = _dot(parts[0], upper) + _dot(parts[1], upper) + _dot(parts[2], upper)
        rq = lax.broadcasted_iota(jnp.int32, (rows, LANES), 0)
        lq = lax.broadcasted_iota(jnp.int32, (rows, LANES), 1)
        qpos = rq & (n_new - 1)
        c_q = jnp.sum(jnp.where(lq == qpos, cum, 0.0), axis=-1, keepdims=True)
        cnew_scr[...] = jnp.broadcast_to(c_q, (rows, LANES))
        bias = c_q - cum
        valid = lq <= qpos
        s_scr[:, 0:LANES] = jnp.zeros((rows, LANES), F32)
        for h in range(n_heads):
            h8 = pl.ds(h * n_new, n_new)
            k_h = kn_ref[:, h * head_dim:(h + 1) * head_dim].astype(BF16)
            s_scr[h8, 0:n_new] = _dot_nt(q_scr[h8, :].astype(BF16), k_h)
        s = jnp.where(valid, s_scr[:, 0:LANES] + bias, -jnp.inf)
        m = jnp.max(s, axis=-1, keepdims=True)
        p = jnp.exp(s - m)
        m_scr[...] = jnp.broadcast_to(m, (rows, LANES))
        l_scr[...] = p
        tail_scr[...] = jnp.zeros((rows, LANES), F32)
        p_scr[:, 0:LANES] = p
        for h in range(n_heads):
            h8 = pl.ds(h * n_new, n_new)
            v_h = vn_ref[:, h * head_dim:(h + 1) * head_dim].astype(BF16)
            head_update(h8, p_scr[h8, 0:n_new].astype(BF16), v_h, True)

    kr = lax.broadcasted_iota(jnp.int32, (page, 2 * page), 0)
    kc = lax.broadcasted_iota(jnp.int32, (page, 2 * page), 1)
    after = jnp.where((kr > kc) | (kc >= page), 1.0, 0.0).astype(BF16)

    def scores(h, carry):
        h8 = pl.ds(pl.multiple_of(h * n_new, n_new), n_new)
        q_h = q_scr[h8, :].astype(BF16)
        for j in range(n_pg):
            k_h = k_refs[j][pl.ds(h, page, stride=n_heads), :].astype(BF16)
            s_scr[h8, j * page:(j + 1) * page] = _dot_nt(q_h, k_h)
        return carry

    lax.fori_loop(0, n_heads, scores, 0)

    tail = tail_scr[...]
    biases = []
    for j in range(n_pg):
        parts = spread(lf_refs[j][...])
        full = _dot(parts[0], after) + _dot(parts[1], after) + _dot(parts[2], after)
        biases.append(full[:, 0:page] + tail)
        tail = tail + full[:, page:2 * page]
    tail_scr[...] = tail
    cnew = cnew_scr[...]
    s_all = [s_scr[:, j * page:(j + 1) * page] + biases[j] + cnew for j in range(n_pg)]
    m_chunk = s_all[0]
    for j in range(1, n_pg):
        m_chunk = jnp.maximum(m_chunk, s_all[j])
    m_old = m_scr[...]
    m_new = jnp.maximum(m_old, jnp.max(m_chunk, axis=-1, keepdims=True))
    alpha = jnp.exp(m_old - m_new)
    m_scr[...] = m_new
    alpha_scr[...] = alpha
    l_new = alpha * l_scr[...]
    for j in range(n_pg):
        p = jnp.exp(s_all[j] - m_new)
        l_new = l_new + p
        p_scr[:, j * page:(j + 1) * page] = p
    l_scr[...] = l_new

    def values(h, carry):
        h8 = pl.ds(pl.multiple_of(h * n_new, n_new), n_new)
        contrib = jnp.zeros((n_new, head_dim), F32)
        for j in range(n_pg):
            v_h = v_refs[j][pl.ds(h, page, stride=n_heads), :].astype(BF16)
            contrib = contrib + _dot(p_scr[h8, j * page:(j + 1) * page].astype(BF16), v_h)
        acc_scr[h8, :] = alpha_scr[h8, 0:head_dim] * acc_scr[h8, :] + contrib
        return carry

    lax.fori_loop(0, n_heads, values, 0)

    @pl.when(step == n_steps - 1)
    def _():
        den = jnp.sum(l_scr[...], axis=-1, keepdims=True)
        res = acc_scr[...] / den
        for h in range(n_heads):
            o_ref[:, h * head_dim:(h + 1) * head_dim] = res[h * n_new:(h + 1) * n_new, :]


def _attn_decode(qkv_s, logf_s, cache_k, cache_v, cache_logf, page_table, layer, n_pg):
    n_layers, n_pool, page, n_heads, head_dim = cache_k.shape
    db, n_pages = page_table.shape
    n_new = qkv_s.shape[0] // db
    d = n_heads * head_dim
    rows = n_heads * n_new
    n_steps = n_pages // n_pg
    ck = cache_k.reshape(n_layers, n_pool, page * n_heads, head_dim)
    cv = cache_v.reshape(n_layers, n_pool, page * n_heads, head_dim)

    def page_map(j):
        return lambda b, s, pt: (layer, pt[b * n_pages + (n_pages - 1 - (s * n_pg + j))], 0, 0)

    kv_spec = lambda j: pl.BlockSpec((None, None, page * n_heads, head_dim), page_map(j))
    lf_spec = lambda j: pl.BlockSpec((None, None, page, n_heads), page_map(j))
    new_spec = lambda c: pl.BlockSpec((n_new, d), lambda b, s, pt: (b, c))
    in_specs = [new_spec(0), new_spec(1), new_spec(2),
                pl.BlockSpec((n_new, n_heads), lambda b, s, pt: (b, 0))]
    in_specs += [kv_spec(j) for j in range(n_pg)] + [kv_spec(j) for j in range(n_pg)]
    in_specs += [lf_spec(j) for j in range(n_pg)]
    kern = functools.partial(_attn_decode_kernel, n_pg=n_pg, n_heads=n_heads, head_dim=head_dim,
                             n_new=n_new, page=page, n_steps=n_steps)
    stat = pltpu.VMEM((rows, LANES), F32)
    grid_spec = pltpu.PrefetchScalarGridSpec(
        num_scalar_prefetch=1,
        grid=(db, n_steps),
        in_specs=in_specs,
        out_specs=pl.BlockSpec((n_new, d), lambda b, s, pt: (b, 0)),
        scratch_shapes=[pltpu.VMEM((rows, head_dim), F32),
                        pltpu.VMEM((rows, n_pg * page), F32),
                        pltpu.VMEM((rows, n_pg * page), F32),
                        pltpu.VMEM((rows, head_dim), F32),
                        stat, stat, stat, stat, stat],
    )
    return pl.pallas_call(
        kern,
        grid_spec=grid_spec,
        out_shape=jax.ShapeDtypeStruct((db * n_new, d), F32),
        compiler_params=_params(48, ("parallel", "arbitrary")),
        name="attn_decode",
    )(page_table.reshape(-1), qkv_s, qkv_s, qkv_s, logf_s,
      *([ck] * n_pg), *([cv] * n_pg), *([cache_logf] * n_pg))


def _s5_kernel(u_ref, lre_ref, lim_ref, ldt_ref, bre_ref, bim_ref, cre_ref, cim_ref, d_ref,
               x0r_ref, x0i_ref, y_ref, fr_ref, fi_ref,
               w_scr, c_scr, ar_scr, ai_scr, xs_scr, sr_scr, si_scr, *, nb, n_tc, sp):
    tc = pl.program_id(1)
    rows = xs_scr.shape[0]

    @pl.when(tc == 0)
    def _():
        lre = lre_ref[...]
        lim = lim_ref[...]
        dt = jnp.exp(ldt_ref[...])
        mag = jnp.exp(lre * dt)
        ar = mag * jnp.cos(lim * dt)
        ai = mag * jnp.sin(lim * dt)
        ar_scr[...] = ar
        ai_scr[...] = ai
        xr = ar - 1.0
        den = lre * lre + lim * lim
        cr = (xr * lre + ai * lim) / den
        ci = (ai * lre - xr * lim) / den
        bre = bre_ref[...]
        bim = bim_ref[...]
        w_scr[:, 0:sp] = (bre * cr - bim * ci).astype(BF16)
        w_scr[:, sp:2 * sp] = (bre * ci + bim * cr).astype(BF16)
        c_scr[0:sp, :] = cre_ref[...].astype(BF16)
        c_scr[sp:2 * sp, :] = (-cim_ref[...]).astype(BF16)
        sr_scr[...] = x0r_ref[...]
        si_scr[...] = x0i_ref[...]

    u = u_ref[...]
    xs_scr[...] = _dot(u.astype(BF16), w_scr[...])
    ar8 = jnp.broadcast_to(ar_scr[...], (8, sp))
    ai8 = jnp.broadcast_to(ai_scr[...], (8, sp))
    low = lax.broadcasted_iota(jnp.int32, (8, sp), 0) < 4

    def body(s, carry):
        pr, pi = carry
        r0 = pl.ds(pl.multiple_of(s * 8, 8), 8)
        br = xs_scr[r0, 0:sp]
        bi = xs_scr[r0, sp:2 * sp]
        er = ar8 * pr - ai8 * pi + br
        ei = ar8 * pi + ai8 * pr + bi
        if nb == 8:
            xs_scr[r0, 0:sp] = er
            xs_scr[r0, sp:2 * sp] = ei
            return er, ei
        er4 = pltpu.roll(er, 4, 0)
        ei4 = pltpu.roll(ei, 4, 0)
        orr = ar8 * er4 - ai8 * ei4 + br
        oi = ar8 * ei4 + ai8 * er4 + bi
        xs_scr[r0, 0:sp] = jnp.where(low, er, orr)
        xs_scr[r0, sp:2 * sp] = jnp.where(low, ei, oi)
        return pltpu.roll(orr, 4, 0), pltpu.roll(oi, 4, 0)

    fr, fi = lax.fori_loop(0, rows // 8, body, (sr_scr[...], si_scr[...]))
    sr_scr[...] = fr
    si_scr[...] = fi
    y = _dot(xs_scr[...].astype(BF16), c_scr[...]) + d_ref[...] * u
    y_ref[...] = jax.nn.gelu(y).astype(y_ref.dtype)

    @pl.when(tc == n_tc - 1)
    def _():
        fr_ref[...] = fr
        fi_ref[...] = fi


def _s5_layout(a_re, a_im, b_re, b_im, c_re, c_im, log_dt, gc):
    g, p = a_re.shape
    c = b_re.shape[-1]
    nch = g // gc
    eye = jnp.eye(gc, dtype=F32)

    def bd_in(b):
        b4 = b.reshape(nch, gc, p, c).transpose(0, 1, 3, 2)
        return (b4[:, :, :, None, :] * eye[None, :, None, :, None]).reshape(nch, gc * c, gc * p)

    def bd_out(cm):
        c4 = cm.reshape(nch, gc, c, p).transpose(0, 1, 3, 2)
        return (c4[:, :, :, None, :] * eye[None, :, None, :, None]).reshape(nch, gc * p, gc * c)

    row = lambda x: x.reshape(nch, 1, gc * p)
    return (row(a_re), row(a_im), row(jnp.repeat(log_dt, p)),
            bd_in(b_re), bd_in(b_im), bd_out(c_re), bd_out(c_im))


def _s5_scan(u_tb, layout, d_skip, x0_re, x0_im, nb, n_tc):
    lre, lim, ldt, bre, bim, cre, cim = layout
    rows_total, d = u_tb.shape
    nch, uc, sp = bre.shape
    rows = rows_total // n_tc
    assert rows % 8 == 0 and nb in (4, 8)
    kern = functools.partial(_s5_kernel, nb=nb, n_tc=n_tc, sp=sp)
    par = lambda r, c: pl.BlockSpec((None, r, c), lambda g, t: (g, 0, 0))
    st_spec = pl.BlockSpec((8, sp), lambda g, t: (0, g))
    return pl.pallas_call(
        kern,
        grid=(nch, n_tc),
        in_specs=[pl.BlockSpec((rows, uc), lambda g, t: (t, g)),
                  par(1, sp), par(1, sp), par(1, sp),
                  par(uc, sp), par(uc, sp), par(sp, uc), par(sp, uc),
                  pl.BlockSpec((1, uc), lambda g, t: (0, g)),
                  st_spec, st_spec],
        out_specs=[pl.BlockSpec((rows, uc), lambda g, t: (t, g)), st_spec, st_spec],
        out_shape=[jax.ShapeDtypeStruct((rows_total, d), BF16),
                   jax.ShapeDtypeStruct(x0_re.shape, F32), jax.ShapeDtypeStruct(x0_re.shape, F32)],
        scratch_shapes=[pltpu.VMEM((uc, 2 * sp), BF16), pltpu.VMEM((2 * sp, uc), BF16),
                        pltpu.VMEM((1, sp), F32), pltpu.VMEM((1, sp), F32),
                        pltpu.VMEM((rows, 2 * sp), F32),
                        pltpu.VMEM((8, sp), F32), pltpu.VMEM((8, sp), F32)],
        compiler_params=_params(48, ("parallel", "arbitrary")),
        name="s5_scan",
    )(u_tb, lre, lim, ldt, bre, bim, cre, cim, d_skip.reshape(1, d), x0_re, x0_im)


def _row_tile(m, cap):
    best = None
    for t in range(16, cap + 1, 16):
        if m % t == 0:
            best = t
    return best if best is not None else m


def _col_tile(n, cap):
    best = LANES
    for t in range(LANES, cap + 1, LANES):
        if n % t == 0:
            best = t
    return best


def _time_chunks(t_len, nb, cap_rows):
    best = 1
    for n in range(1, t_len + 1):
        if t_len % n:
            continue
        steps = t_len // n
        rows = steps * nb
        if rows % 16 == 0 and rows <= cap_rows:
            return n
        best = n
    return best


def kernel(x_prompt, x_sample, cache_k, cache_v, cache_logf, state_s5_re, state_s5_im, page_table,
           meta_tokens, norm_g, ffn_w_up, ffn_w_down, attn_w_in, attn_b_f, attn_w_out,
           s5_A_re, s5_A_im, s5_B_re, s5_B_im, s5_C_re, s5_C_im, s5_log_dt, s5_D, s5_w_glu):
    bsz, seq, d = x_prompt.shape
    db, ds, _ = x_sample.shape
    n_meta = meta_tokens.shape[0]
    t_len = n_meta + seq
    mp, ms = bsz * t_len, db * ds
    m = mp + ms
    depth = norm_g.shape[0]
    n_heads = attn_b_f.shape[-1]
    n_groups, state_dim = s5_A_re.shape[1:]

    tm_small = _row_tile(m, 640)
    tm_big = _row_tile(m, 832)
    up_chunk = _row_tile(m, 640)
    n_tc = _time_chunks(t_len, bsz, 1400)
    tn_qkv = _col_tile(3 * d, 512)
    tn_out = _col_tile(d, 512)
    tq = 256 if seq % 256 == 0 else LANES

    w_down = ffn_w_down.astype(BF16)
    w_in = attn_w_in.astype(BF16)
    w_out = attn_w_out.astype(BF16)
    w_glu = s5_w_glu.astype(BF16)

    meta = jnp.broadcast_to(meta_tokens[None].astype(x_prompt.dtype), (bsz, n_meta, d))
    hp = jnp.concatenate([meta, x_prompt], axis=1).reshape(mp, d)
    h = jnp.concatenate([hp, x_sample.reshape(ms, d)], axis=0)

    def to_time_major(x):
        xp = x[:mp].reshape(bsz, t_len, d).transpose(1, 0, 2).reshape(mp, d)
        xs = x[mp:].reshape(db, ds, d).transpose(1, 0, 2).reshape(ms, d)
        return xp, xs

    def to_batch_major(xp, xs):
        xp = xp.reshape(t_len, bsz, d).transpose(1, 0, 2).reshape(mp, d)
        xs = xs.reshape(ds, db, d).transpose(1, 0, 2).reshape(ms, d)
        return jnp.concatenate([xp, xs], axis=0)

    k_pr, v_pr, lf_pr, sre_pr, sim_pr = [], [], [], [], []
    k_sa, v_sa, lf_sa, sre_sa, sim_sa = [], [], [], [], []
    u = _rmsnorm(h, norm_g[0, 0], tm_big)
    for i in range(depth):
        g = norm_g[i]
        attn_layer = i % 2 == 0
        act = _ffn_up(u, ffn_w_up, i, 0, up_chunk)
        h, u = _ffn_down(act, w_down, i, 0, h, g[1], g[2], BF16 if attn_layer else F32, tm_small)
        if attn_layer:
            li = i // 2
            qkv, logf = _qkv_proj(u, w_in, attn_b_f, li, tm_big, tn_qkv)
            cparts = _prompt_cumsum(logf, bsz, t_len)
            yp = _attn_prompt(qkv, cparts, bsz, t_len, n_meta, n_heads, tq)
            ys = _attn_decode(qkv[mp:], logf[mp:], cache_k, cache_v, cache_logf, page_table, li, 2)
            y = jnp.concatenate([yp, ys.astype(BF16)], axis=0)
            h, u = _mixer_out(y, w_out, li, False, h, g[3], g[4], tm_small, tn_out)
            k_pr.append(qkv[:mp, d:2 * d]); v_pr.append(qkv[:mp, 2 * d:]); lf_pr.append(logf[:mp])
            k_sa.append(qkv[mp:, d:2 * d]); v_sa.append(qkv[mp:, 2 * d:]); lf_sa.append(logf[mp:])
        else:
            si = i // 2
            layout = _s5_layout(s5_A_re[si], s5_A_im[si], s5_B_re[si], s5_B_im[si],
                                s5_C_re[si], s5_C_im[si], s5_log_dt[si], 16)
            up_tb, us_tb = to_time_major(u)
            zeros = jnp.zeros((8, n_groups * state_dim), F32)
            yp, re_p, im_p = _s5_scan(up_tb, layout, s5_D[si], zeros, zeros, bsz, n_tc)
            ys, re_s, im_s = _s5_scan(us_tb, layout, s5_D[si],
                                      state_s5_re[si].reshape(db, -1), state_s5_im[si].reshape(db, -1),
                                      db, 1)
            y = to_batch_major(yp, ys)
            h, u = _mixer_out(y, w_glu, si, True, h, g[3], g[4], tm_small, tn_out)
            sre_pr.append(re_p[:bsz]); sim_pr.append(im_p[:bsz])
            sre_sa.append(re_s[:db]); sim_sa.append(im_s[:db])
        act = _ffn_up(u, ffn_w_up, i, 1, up_chunk)
        g_next = norm_g[i + 1, 0] if i + 1 < depth else None
        h, u = _ffn_down(act, w_down, i, 1, h, g[5], g_next, BF16, tm_small)

    hd = d // n_heads
    stack = lambda xs, shape: jnp.stack(xs).reshape((len(xs),) + shape)
    return (h[:mp].reshape(bsz, t_len, d)[:, n_meta:], h[mp:].reshape(db, ds, d),
            stack(k_pr, (bsz, t_len, n_heads, hd)), stack(v_pr, (bsz, t_len, n_heads, hd)),
            stack(lf_pr, (bsz, t_len, n_heads)),
            stack(sre_pr, (bsz, n_groups, state_dim)), stack(sim_pr, (bsz, n_groups, state_dim)),
            stack(k_sa, (db, ds, n_heads, hd)), stack(v_sa, (db, ds, n_heads, hd)),
            stack(lf_sa, (db, ds, n_heads)),
            stack(sre_sa, (db, n_groups, state_dim)), stack(sim_sa, (db, n_groups, state_dim)))
```

```python
import functools

import jax
import jax.numpy as jnp
from jax import lax
from jax.experimental import pallas as pl
from jax.experimental.pallas import tpu as pltpu

F32 = jnp.float32
BF16 = jnp.bfloat16
RMS_EPS = 1e-6
LANES = 128
MIB = 1024 * 1024
NT_DIMS = (((1,), (1,)), ((), ()))


def _params(vmem_mib, semantics=None):
    return pltpu.CompilerParams(dimension_semantics=semantics, vmem_limit_bytes=vmem_mib * MIB)


def _rms(x, g):
    ms = jnp.mean(x * x, axis=-1, keepdims=True)
    return (x * lax.rsqrt(ms + RMS_EPS)) * g


def _split3(x):
    hi = x.astype(BF16)
    r1 = x - hi.astype(F32)
    mid = r1.astype(BF16)
    lo = (r1 - mid.astype(F32)).astype(BF16)
    return hi, mid, lo


def _dot(a, b):
    return jnp.dot(a, b, preferred_element_type=F32)


def _dot_nt(a, b):
    return lax.dot_general(a, b, NT_DIMS, preferred_element_type=F32)


def _rmsnorm_kernel(h_ref, g_ref, o_ref):
    o_ref[...] = _rms(h_ref[...], g_ref[...]).astype(o_ref.dtype)


def _rmsnorm(h, g, tm):
    m, d = h.shape
    return pl.pallas_call(
        _rmsnorm_kernel,
        grid=(m // tm,),
        in_specs=[pl.BlockSpec((tm, d), lambda i: (i, 0)), pl.BlockSpec((1, d), lambda i: (0, 0))],
        out_specs=pl.BlockSpec((tm, d), lambda i: (i, 0)),
        out_shape=jax.ShapeDtypeStruct((m, d), BF16),
        compiler_params=_params(40),
        name="rmsnorm",
    )(h, g.reshape(1, d))


def _ffn_up_kernel(u_ref, wa_ref, wb_ref, o_ref, w_scr, *, n_chunks, ch):
    w_scr[:, :LANES] = wa_ref[...].astype(BF16)
    w_scr[:, LANES:] = wb_ref[...].astype(BF16)

    def body(c, carry):
        rows = pl.ds(pl.multiple_of(c * ch, ch), ch)
        z = _dot(u_ref[rows, :], w_scr[...])
        a = z[:, :LANES]
        b = z[:, LANES:]
        o_ref[rows, :] = (a * jax.nn.sigmoid(a) * b).astype(o_ref.dtype)
        return carry

    lax.fori_loop(0, n_chunks, body, 0, unroll=True)


def _ffn_up(u, w_up, layer, half, ch):
    m, d = u.shape
    f = w_up.shape[-1] // 2
    nt = f // LANES
    kern = functools.partial(_ffn_up_kernel, n_chunks=m // ch, ch=ch)
    return pl.pallas_call(
        kern,
        grid=(nt,),
        in_specs=[
            pl.BlockSpec(memory_space=pltpu.VMEM),
            pl.BlockSpec((None, None, d, LANES), lambda j: (layer, half, 0, j)),
            pl.BlockSpec((None, None, d, LANES), lambda j: (layer, half, 0, j + nt)),
        ],
        out_specs=pl.BlockSpec((m, LANES), lambda j: (0, j)),
        out_shape=jax.ShapeDtypeStruct((m, f), BF16),
        scratch_shapes=[pltpu.VMEM((d, 2 * LANES), BF16)],
        compiler_params=_params(56),
        name="ffn_up",
    )(u, w_up, w_up)


def _residual_epilogue(y, h_ref, gp_ref, gn_ref, hn_ref, un_ref, scale):
    hn = h_ref[...] + scale * _rms(y, gp_ref[...])
    hn_ref[...] = hn
    if un_ref is not None:
        un_ref[...] = _rms(hn, gn_ref[...]).astype(un_ref.dtype)


def _down_tiling(f):
    nt = f // LANES
    for dmul in (8, 6, 4, 2, 7, 5, 3):
        if nt % dmul == 0 and nt // dmul >= 1:
            return dmul * LANES, nt // dmul, False
    for dmul in (8, 6, 4, 2, 7, 5, 3):
        if (nt - 1) % dmul == 0:
            return dmul * LANES, (nt - 1) // dmul, True
    return LANES, nt, False


def _ffn_down_kernel(*refs, nk, scale, has_tail, has_next):
    refs = list(refs)
    xm_ref = refs.pop(0)
    xt_ref = refs.pop(0) if has_tail else None
    wm_ref = refs.pop(0)
    wt_ref = refs.pop(0) if has_tail else None
    h_ref = refs.pop(0)
    gp_ref = refs.pop(0)
    gn_ref = refs.pop(0) if has_next else None
    hn_ref = refs.pop(0)
    un_ref = refs.pop(0) if has_next else None
    acc_ref = refs.pop(0)
    k = pl.program_id(1)

    @pl.when(k == 0)
    def _():
        if has_tail:
            acc_ref[...] = _dot(xt_ref[...], wt_ref[...])
        else:
            acc_ref[...] = jnp.zeros(acc_ref.shape, F32)

    acc_ref[...] += _dot(xm_ref[...], wm_ref[...])

    @pl.when(k == nk - 1)
    def _():
        _residual_epilogue(acc_ref[...], h_ref, gp_ref, gn_ref, hn_ref, un_ref, scale)


def _ffn_down(act, w_down, layer, half, h, g_post, g_next, u_dtype, tm):
    m, f = act.shape
    d = h.shape[1]
    tk, nk, has_tail = _down_tiling(f)
    has_next = g_next is not None
    tail_blk = (f - LANES) // LANES
    in_specs = [pl.BlockSpec((tm, tk), lambda i, k: (i, k))]
    args = [act]
    if has_tail:
        in_specs.append(pl.BlockSpec((tm, LANES), lambda i, k: (i, tail_blk)))
        args.append(act)
    in_specs.append(pl.BlockSpec((None, None, tk, d), lambda i, k: (layer, half, k, 0)))
    args.append(w_down)
    if has_tail:
        in_specs.append(pl.BlockSpec((None, None, LANES, d), lambda i, k: (layer, half, tail_blk, 0)))
        args.append(w_down)
    row_spec = pl.BlockSpec((tm, d), lambda i, k: (i, 0))
    vec_spec = pl.BlockSpec((1, d), lambda i, k: (0, 0))
    in_specs += [row_spec, vec_spec]
    args += [h, g_post.reshape(1, d)]
    out_specs = [row_spec]
    out_shape = [jax.ShapeDtypeStruct((m, d), F32)]
    if has_next:
        in_specs.append(vec_spec)
        args.append(g_next.reshape(1, d))
        out_specs.append(row_spec)
        out_shape.append(jax.ShapeDtypeStruct((m, d), u_dtype))
    kern = functools.partial(_ffn_down_kernel, nk=nk, scale=0.5, has_tail=has_tail, has_next=has_next)
    res = pl.pallas_call(
        kern,
        grid=(m // tm, nk),
        in_specs=in_specs,
        out_specs=out_specs,
        out_shape=out_shape,
        scratch_shapes=[pltpu.VMEM((tm, d), F32)],
        compiler_params=_params(56, ("parallel", "arbitrary")),
        name="ffn_down",
    )(*args)
    return (res[0], res[1]) if has_next else (res[0], None)


def _proj_res_kernel(x_ref, w_ref, h_ref, gp_ref, gn_ref, hn_ref, un_ref, acc_ref, *, nn, tn):
    j = pl.program_id(1)
    col = pl.multiple_of(j * tn, tn)
    acc_ref[:, pl.ds(col, tn)] = _dot(x_ref[...], w_ref[...])

    @pl.when(j == nn - 1)
    def _():
        _residual_epilogue(acc_ref[...], h_ref, gp_ref, gn_ref, hn_ref, un_ref, 1.0)


def _glu_res_kernel(x_ref, wa_ref, wb_ref, h_ref, gp_ref, gn_ref, hn_ref, un_ref, acc_ref, *, nn, tn):
    j = pl.program_id(1)
    col = pl.multiple_of(j * tn, tn)
    x = x_ref[...]
    a = _dot(x, wa_ref[...])
    b = _dot(x, wb_ref[...])
    acc_ref[:, pl.ds(col, tn)] = a * jax.nn.sigmoid(b)

    @pl.when(j == nn - 1)
    def _():
        _residual_epilogue(acc_ref[...], h_ref, gp_ref, gn_ref, hn_ref, un_ref, 1.0)


def _mixer_out(x, w, layer, glu, h, g_post, g_next, tm, tn):
    m, kdim = x.shape
    d = h.shape[1]
    nn = d // tn
    row_spec = pl.BlockSpec((tm, d), lambda i, j: (i, 0))
    vec_spec = pl.BlockSpec((1, d), lambda i, j: (0, 0))
    in_specs = [pl.BlockSpec((tm, kdim), lambda i, j: (i, 0)),
                pl.BlockSpec((None, kdim, tn), lambda i, j: (layer, 0, j))]
    args = [x, w]
    if glu:
        in_specs.append(pl.BlockSpec((None, kdim, tn), lambda i, j: (layer, 0, j + nn)))
        args.append(w)
    in_specs += [row_spec, vec_spec, vec_spec]
    args += [h, g_post.reshape(1, d), g_next.reshape(1, d)]
    body = _glu_res_kernel if glu else _proj_res_kernel
    return pl.pallas_call(
        functools.partial(body, nn=nn, tn=tn),
        grid=(m // tm, nn),
        in_specs=in_specs,
        out_specs=[row_spec, row_spec],
        out_shape=[jax.ShapeDtypeStruct((m, d), F32), jax.ShapeDtypeStruct((m, d), BF16)],
        scratch_shapes=[pltpu.VMEM((tm, d), F32)],
        compiler_params=_params(56, ("parallel", "arbitrary")),
        name="mixer_out_glu" if glu else "mixer_out",
    )(*args)


def _log_sigmoid(x):
    return jnp.minimum(x, 0.0) - jnp.log1p(jnp.exp(-jnp.abs(x)))


def _qkv_kernel(x_ref, w_ref, wf_ref, bf_ref, o_ref, lf_ref, *, n_heads):
    j = pl.program_id(1)
    x = x_ref[...]
    o_ref[...] = _dot(x, w_ref[...])

    @pl.when(j == 0)
    def _():
        f = _dot(x, wf_ref[...])[:, :n_heads] + bf_ref[...]
        lf_ref[...] = _log_sigmoid(f)


def _qkv_proj(u, w_in, b_f, layer, tm, tn):
    m, d = u.shape
    n_heads = b_f.shape[-1]
    n_main = 3 * d
    return pl.pallas_call(
        functools.partial(_qkv_kernel, n_heads=n_heads),
        grid=(m // tm, n_main // tn),
        in_specs=[
            pl.BlockSpec((tm, d), lambda i, j: (i, 0)),
            pl.BlockSpec((None, d, tn), lambda i, j: (layer, 0, j)),
            pl.BlockSpec((None, d, LANES), lambda i, j: (layer, 0, n_main // LANES)),
            pl.BlockSpec((None, 1, n_heads), lambda i, j: (layer, 0, 0)),
        ],
        out_specs=[pl.BlockSpec((tm, tn), lambda i, j: (i, j)),
                   pl.BlockSpec((tm, n_heads), lambda i, j: (i, 0))],
        out_shape=[jax.ShapeDtypeStruct((m, n_main), F32), jax.ShapeDtypeStruct((m, n_heads), F32)],
        compiler_params=_params(48, ("parallel", "arbitrary")),
        name="qkv_proj",
    )(u, w_in, w_in, b_f.reshape(b_f.shape[0], 1, n_heads))


def _cumsum_kernel(lf_ref, o_ref, *, t_len, chunk, n_heads):
    r = lax.broadcasted_iota(jnp.int32, (chunk, chunk), 0)
    c = lax.broadcasted_iota(jnp.int32, (chunk, chunk), 1)
    tri = jnp.where(c <= r, 1.0, 0.0).astype(BF16)
    carry = jnp.zeros((1, n_heads), F32)
    for s in range(t_len // chunk):
        x = lf_ref[s * chunk:(s + 1) * chunk, :]
        hi, mid, lo = _split3(x)
        cs = _dot(tri, hi) + _dot(tri, mid) + _dot(tri, lo) + carry
        carry = cs[chunk - 1:chunk, :]
        chi, cmid, clo = _split3(cs)
        o_ref[s * chunk:(s + 1) * chunk, 0:n_heads] = chi
        o_ref[s * chunk:(s + 1) * chunk, n_heads:2 * n_heads] = cmid
        o_ref[s * chunk:(s + 1) * chunk, 2 * n_heads:3 * n_heads] = clo


def _cumsum_chunk(t_len):
    for c in (688, 256, 128, 64, 48, 16):
        if t_len % c == 0:
            return c
    return t_len


def _prompt_cumsum(logf, batch, t_len):
    n_heads = logf.shape[1]
    kern = functools.partial(_cumsum_kernel, t_len=t_len, chunk=_cumsum_chunk(t_len), n_heads=n_heads)
    return pl.pallas_call(
        kern,
        grid=(batch,),
        in_specs=[pl.BlockSpec((t_len, n_heads), lambda b: (b, 0))],
        out_specs=pl.BlockSpec((t_len, 3 * n_heads), lambda b: (b, 0)),
        out_shape=jax.ShapeDtypeStruct((batch * t_len, 3 * n_heads), BF16),
        compiler_params=_params(32, ("parallel",)),
        name="prompt_cumsum",
    )(logf)


def _attn_prompt_kernel(q_ref, k_ref, v_ref, cp_ref, o_ref, qa_scr, ka_scr, v_scr,
                        *, t_len, n_meta, tq, n_heads, head_dim):
    hp = pl.program_id(1)
    half = LANES // 2
    lane = lax.broadcasted_iota(jnp.int32, (1, LANES), 1)
    r3 = lax.broadcasted_iota(jnp.int32, (3 * n_heads, LANES), 0)
    l3 = lax.broadcasted_iota(jnp.int32, (3 * n_heads, LANES), 1)
    cp = cp_ref[...]
    q = (q_ref[...] * (head_dim ** -0.5)).astype(BF16)
    k = k_ref[...].astype(BF16)
    v_scr[...] = v_ref[...].astype(BF16)
    for hh in range(2):
        head = 2 * hp + hh
        base = half * (1 - hh)
        sel_q = jnp.zeros((3 * n_heads, LANES), F32)
        sel_k = jnp.zeros((3 * n_heads, LANES), F32)
        for part in range(3):
            row_hit = r3 == head + part * n_heads
            sel_q = sel_q + jnp.where(row_hit & (l3 == base + part), 1.0, 0.0)
            sel_k = sel_k + jnp.where(row_hit & (l3 == base + 3 + part), -1.0, 0.0)
        ones_q = jnp.where((lane >= base + 3) & (lane < base + 6), 1.0, 0.0)
        ones_k = jnp.where((lane >= base) & (lane < base + 3), 1.0, 0.0)
        ex_q = (_dot(cp, sel_q.astype(BF16)) + ones_q).astype(BF16)
        ex_k = (_dot(cp, sel_k.astype(BF16)) + ones_k).astype(BF16)
        data = (lane >= half * hh) & (lane < half * hh + half)
        qa_scr[hh] = jnp.where(data, q, ex_q)
        ka_scr[hh] = jnp.where(data, k, ex_k)

    tiles = [(0, n_meta)] + [(n_meta + i * tq, tq) for i in range((t_len - n_meta) // tq)]
    for q0, tl in tiles:
        row = lax.broadcasted_iota(jnp.int32, (tl, tl), 0)
        col = lax.broadcasted_iota(jnp.int32, (tl, tl), 1)
        causal = col <= row
        outs = []
        for hh in range(2):
            qt = qa_scr[hh, q0:q0 + tl, :]
            sd = jnp.where(causal, _dot_nt(qt, ka_scr[hh, q0:q0 + tl, :]), -jnp.inf)
            m = jnp.max(sd, axis=-1, keepdims=True)
            if q0 > 0:
                so = _dot_nt(qt, ka_scr[hh, 0:q0, :])
                m = jnp.maximum(m, jnp.max(so, axis=-1, keepdims=True))
            pd = jnp.exp(sd - m)
            den = jnp.sum(pd, axis=-1, keepdims=True)
            o = _dot(pd.astype(BF16), v_scr[q0:q0 + tl, :])
            if q0 > 0:
                po = jnp.exp(so - m)
                den = den + jnp.sum(po, axis=-1, keepdims=True)
                o = o + _dot(po.astype(BF16), v_scr[0:q0, :])
            outs.append(o / den)
        o_ref[q0:q0 + tl, :] = jnp.where(lane < half, outs[0], outs[1]).astype(o_ref.dtype)


def _attn_prompt(qkv, cparts, batch, t_len, n_meta, n_heads, tq):
    d = qkv.shape[1] // 3
    head_dim = d // n_heads
    assert 2 * head_dim == LANES, "head pairs must fill one lane tile"
    assert (t_len - n_meta) % tq == 0
    npair = n_heads // 2
    kern = functools.partial(_attn_prompt_kernel, t_len=t_len, n_meta=n_meta, tq=tq,
                             n_heads=n_heads, head_dim=head_dim)
    blk = lambda off: pl.BlockSpec((t_len, LANES), lambda b, p: (b, off + p))
    return pl.pallas_call(
        kern,
        grid=(batch, npair),
        in_specs=[blk(0), blk(npair), blk(2 * npair),
                  pl.BlockSpec((t_len, 3 * n_heads), lambda b, p: (b, 0))],
        out_specs=pl.BlockSpec((t_len, LANES), lambda b, p: (b, p)),
        out_shape=jax.ShapeDtypeStruct((batch * t_len, d), BF16),
        scratch_shapes=[pltpu.VMEM((2, t_len, LANES), BF16), pltpu.VMEM((2, t_len, LANES), BF16),
                        pltpu.VMEM((t_len, LANES), BF16)],
        compiler_params=_params(48, ("parallel", "arbitrary")),
        name="attn_prompt",
    )(qkv, qkv, qkv, cparts)


def _attn_decode_kernel(pt_ref, q_ref, kn_ref, vn_ref, lfn_ref, *rest,
                        n_pg, n_heads, head_dim, n_new, page, n_steps):
    del pt_ref
    k_refs = rest[:n_pg]
    v_refs = rest[n_pg:2 * n_pg]
    lf_refs = rest[2 * n_pg:3 * n_pg]
    o_ref = rest[3 * n_pg]
    (q_scr, s_scr, p_scr, acc_scr, m_scr, l_scr, cnew_scr, alpha_scr,
     tail_scr, b_scr) = rest[3 * n_pg + 1:]
    step = pl.program_id(1)
    rows = n_heads * n_new
    head_unroll = 8 if n_heads % 8 == 0 else 1
    er = lax.broadcasted_iota(jnp.int32, (rows, n_heads), 0)
    ec = lax.broadcasted_iota(jnp.int32, (rows, n_heads), 1)
    expand = jnp.where((er >= ec * n_new) & (er < (ec + 1) * n_new), 1.0, 0.0).astype(BF16)

    def spread(lf):
        return [_dot_nt(expand, part).astype(BF16) for part in _split3(lf)]

    def head_update(h8, p_h, v_h, first):
        contrib = _dot(p_h, v_h)
        if first:
            acc_scr[h8, :] = contrib
        else:
            acc_scr[h8, :] = alpha_scr[h8, 0:head_dim] * acc_scr[h8, :] + contrib

    @pl.when(step == 0)
    def _():
        scale = head_dim ** -0.5
        for h in range(n_heads):
            q_scr[h * n_new:(h + 1) * n_new, :] = q_ref[:, h * head_dim:(h + 1) * head_dim] * scale
        kr = lax.broadcasted_iota(jnp.int32, (n_new, LANES), 0)
        kc = lax.broadcasted_iota(jnp.int32, (n_new, LANES), 1)
        upper = jnp.where(kr <= kc, 1.0, 0.0).astype(BF16)
        parts = spread(lfn_ref[...])
        cum = _dot(parts[0], upper) + _dot(parts[1], upper) + _dot(parts[2], upper)
        rq = lax.broadcasted_iota(jnp.int32, (rows, LANES), 0)
        lq = lax.broadcasted_iota(jnp.int32, (rows, LANES), 1)
        qpos = rq & (n_new - 1)
        c_q = jnp.sum(jnp.where(lq == qpos, cum, 0.0), axis=-1, keepdims=True)
        cnew_scr[...] = jnp.broadcast_to(c_q, (rows, LANES))
        bias = c_q - cum
        valid = lq <= qpos
        s_scr[:, 0:LANES] = jnp.zeros((rows, LANES), F32)
        for h in range(n_heads):
            h8 = pl.ds(h * n_new, n_new)
            k_h = kn_ref[:, h * head_dim:(h + 1) * head_dim].astype(BF16)
            s_scr[h8, 0:n_new] = _dot_nt(q_scr[h8, :].astype(BF16), k_h)
        s = jnp.where(valid, s_scr[:, 0:LANES] + bias, -jnp.inf)
        m = jnp.max(s, axis=-1, keepdims=True)
        p = jnp.exp(s - m)
        m_scr[...] = jnp.broadcast_to(m, (rows, LANES))
        l_scr[...] = p
        tail_scr[...] = jnp.zeros((n_heads, LANES), F32)
        p_scr[:, 0:LANES] = p
        for h in range(n_heads):
            h8 = pl.ds(h * n_new, n_new)
            v_h = vn_ref[:, h * head_dim:(h + 1) * head_dim].astype(BF16)
            head_update(h8, p_scr[h8, 0:n_new].astype(BF16), v_h, True)

    kr = lax.broadcasted_iota(jnp.int32, (page, 2 * page), 0)
    kc = lax.broadcasted_iota(jnp.int32, (page, 2 * page), 1)
    after = jnp.where((kr > kc) | (kc >= page), 1.0, 0.0).astype(BF16)

    tail = tail_scr[...]
    for j in range(n_pg):
        hi, mid, lo = _split3(lf_refs[j][...])
        full = _dot(hi, after) + _dot(mid, after) + _dot(lo, after)
        b_scr[:, j * page:(j + 1) * page] = full[:, 0:page] + tail
        tail = tail + full[:, page:2 * page]
    tail_scr[...] = tail

    def scores(h, carry):
        h8 = pl.ds(pl.multiple_of(h * n_new, n_new), n_new)
        q_h = q_scr[h8, :].astype(BF16)
        k_h = jnp.concatenate([k_refs[j][h] for j in range(n_pg)], axis=1).astype(BF16)
        s_scr[h8, :] = _dot(q_h, k_h) + b_scr[pl.ds(h, 1), :]
        return carry

    lax.fori_loop(0, n_heads, scores, 0, unroll=head_unroll)

    cnew = cnew_scr[...]
    s_all = [s_scr[:, j * page:(j + 1) * page] + cnew for j in range(n_pg)]
    m_chunk = s_all[0]
    for j in range(1, n_pg):
        m_chunk = jnp.maximum(m_chunk, s_all[j])
    m_old = m_scr[...]
    m_new = jnp.maximum(m_old, jnp.max(m_chunk, axis=-1, keepdims=True))
    alpha = jnp.exp(m_old - m_new)
    m_scr[...] = m_new
    alpha_scr[...] = alpha
    l_new = alpha * l_scr[...]
    for j in range(n_pg):
        p = jnp.exp(s_all[j] - m_new)
        l_new = l_new + p
        p_scr[:, j * page:(j + 1) * page] = p
    l_scr[...] = l_new

    def values(h, carry):
        h8 = pl.ds(pl.multiple_of(h * n_new, n_new), n_new)
        v_h = jnp.concatenate([v_refs[j][h] for j in range(n_pg)], axis=1).astype(BF16)
        contrib = _dot_nt(p_scr[h8, :].astype(BF16), v_h)
        acc_scr[h8, :] = alpha_scr[h8, 0:head_dim] * acc_scr[h8, :] + contrib
        return carry

    lax.fori_loop(0, n_heads, values, 0, unroll=head_unroll)

    @pl.when(step == n_steps - 1)
    def _():
        den = jnp.sum(l_scr[...], axis=-1, keepdims=True)
        res = acc_scr[...] / den
        for h in range(n_heads):
            o_ref[:, h * head_dim:(h + 1) * head_dim] = res[h * n_new:(h + 1) * n_new, :]


def _attn_decode(qkv_s, logf_s, cache_k, cache_v, cache_logf, page_table, layer, n_pg):
    n_layers, n_pool, page, n_heads, head_dim = cache_k.shape
    db, n_pages = page_table.shape
    n_new = qkv_s.shape[0] // db
    d = n_heads * head_dim
    rows = n_heads * n_new
    n_steps = n_pages // n_pg
    ck = cache_k.transpose(0, 1, 3, 4, 2)
    cv = cache_v.transpose(0, 1, 3, 4, 2)
    clf = cache_logf.transpose(0, 1, 3, 2)

    def page_of(b, s, pt, j):
        return pt[b * n_pages + (n_pages - 1 - (s * n_pg + j))]

    kv_spec = lambda j: pl.BlockSpec((None, None, n_heads, head_dim, page),
                                     lambda b, s, pt: (layer, page_of(b, s, pt, j), 0, 0, 0))
    lf_spec = lambda j: pl.BlockSpec((None, None, n_heads, page),
                                     lambda b, s, pt: (layer, page_of(b, s, pt, j), 0, 0))
    new_spec = lambda c: pl.BlockSpec((n_new, d), lambda b, s, pt: (b, c))
    in_specs = [new_spec(0), new_spec(1), new_spec(2),
                pl.BlockSpec((n_new, n_heads), lambda b, s, pt: (b, 0))]
    in_specs += [kv_spec(j) for j in range(n_pg)] + [kv_spec(j) for j in range(n_pg)]
    in_specs += [lf_spec(j) for j in range(n_pg)]
    kern = functools.partial(_attn_decode_kernel, n_pg=n_pg, n_heads=n_heads, head_dim=head_dim,
                             n_new=n_new, page=page, n_steps=n_steps)
    stat = pltpu.VMEM((rows, LANES), F32)
    grid_spec = pltpu.PrefetchScalarGridSpec(
        num_scalar_prefetch=1,
        grid=(db, n_steps),
        in_specs=in_specs,
        out_specs=pl.BlockSpec((n_new, d), lambda b, s, pt: (b, 0)),
        scratch_shapes=[pltpu.VMEM((rows, head_dim), F32),
                        pltpu.VMEM((rows, n_pg * page), F32),
                        pltpu.VMEM((rows, n_pg * page), F32),
                        pltpu.VMEM((rows, head_dim), F32),
                        stat, stat, stat, stat,
                        pltpu.VMEM((n_heads, LANES), F32),
                        pltpu.VMEM((n_heads, n_pg * page), F32)],
    )
    return pl.pallas_call(
        kern,
        grid_spec=grid_spec,
        out_shape=jax.ShapeDtypeStruct((db * n_new, d), F32),
        compiler_params=_params(48, ("parallel", "arbitrary")),
        name="attn_decode",
    )(page_table.reshape(-1), qkv_s, qkv_s, qkv_s, logf_s,
      *([ck] * n_pg), *([cv] * n_pg), *([clf] * n_pg))


def _s5_kernel(u_ref, lre_ref, lim_ref, ldt_ref, bre_ref, bim_ref, cre_ref, cim_ref, d_ref,
               x0r_ref, x0i_ref, y_ref, fr_ref, fi_ref,
               w_scr, c_scr, ar_scr, ai_scr, xs_scr, sr_scr, si_scr, *, nb, n_tc, sp):
    tc = pl.program_id(1)
    rows = xs_scr.shape[0]

    @pl.when(tc == 0)
    def _():
        lre = lre_ref[...]
        lim = lim_ref[...]
        dt = jnp.exp(ldt_ref[...])
        mag = jnp.exp(lre * dt)
        ar = mag * jnp.cos(lim * dt)
        ai = mag * jnp.sin(lim * dt)
        ar_scr[...] = ar
        ai_scr[...] = ai
        xr = ar - 1.0
        den = lre * lre + lim * lim
        cr = (xr * lre + ai * lim) / den
        ci = (ai * lre - xr * lim) / den
        bre = bre_ref[...]
        bim = bim_ref[...]
        w_scr[:, 0:sp] = (bre * cr - bim * ci).astype(BF16)
        w_scr[:, sp:2 * sp] = (bre * ci + bim * cr).astype(BF16)
        c_scr[0:sp, :] = cre_ref[...].astype(BF16)
        c_scr[sp:2 * sp, :] = (-cim_ref[...]).astype(BF16)
        sr_scr[...] = x0r_ref[...]
        si_scr[...] = x0i_ref[...]

    u = u_ref[...]
    xs_scr[...] = _dot(u.astype(BF16), w_scr[...])
    ar8 = jnp.broadcast_to(ar_scr[...], (8, sp))
    ai8 = jnp.broadcast_to(ai_scr[...], (8, sp))
    low = lax.broadcasted_iota(jnp.int32, (8, sp), 0) < 4

    def body(s, carry):
        pr, pi = carry
        r0 = pl.ds(pl.multiple_of(s * 8, 8), 8)
        br = xs_scr[r0, 0:sp]
        bi = xs_scr[r0, sp:2 * sp]
        er = ar8 * pr - ai8 * pi + br
        ei = ar8 * pi + ai8 * pr + bi
        if nb == 8:
            xs_scr[r0, 0:sp] = er
            xs_scr[r0, sp:2 * sp] = ei
            return er, ei
        er4 = pltpu.roll(er, 4, 0)
        ei4 = pltpu.roll(ei, 4, 0)
        orr = ar8 * er4 - ai8 * ei4 + br
        oi = ar8 * ei4 + ai8 * er4 + bi
        xs_scr[r0, 0:sp] = jnp.where(low, er, orr)
        xs_scr[r0, sp:2 * sp] = jnp.where(low, ei, oi)
        return pltpu.roll(orr, 4, 0), pltpu.roll(oi, 4, 0)

    fr, fi = lax.fori_loop(0, rows // 8, body, (sr_scr[...], si_scr[...]))
    sr_scr[...] = fr
    si_scr[...] = fi
    y = _dot(xs_scr[...].astype(BF16), c_scr[...]) + d_ref[...] * u
    y_ref[...] = jax.nn.gelu(y).astype(y_ref.dtype)

    @pl.when(tc == n_tc - 1)
    def _():
        fr_ref[...] = fr
        fi_ref[...] = fi


def _s5_layout(a_re, a_im, b_re, b_im, c_re, c_im, log_dt, gc):
    g, p = a_re.shape
    c = b_re.shape[-1]
    nch = g // gc
    eye = jnp.eye(gc, dtype=F32)

    def bd_in(b):
        b4 = b.reshape(nch, gc, p, c).transpose(0, 1, 3, 2)
        return (b4[:, :, :, None, :] * eye[None, :, None, :, None]).reshape(nch, gc * c, gc * p)

    def bd_out(cm):
        c4 = cm.reshape(nch, gc, c, p).transpose(0, 1, 3, 2)
        return (c4[:, :, :, None, :] * eye[None, :, None, :, None]).reshape(nch, gc * p, gc * c)

    row = lambda x: x.reshape(nch, 1, gc * p)
    return (row(a_re), row(a_im), row(jnp.repeat(log_dt, p)),
            bd_in(b_re), bd_in(b_im), bd_out(c_re), bd_out(c_im))


def _s5_scan(u_tb, layout, d_skip, x0_re, x0_im, nb, n_tc):
    lre, lim, ldt, bre, bim, cre, cim = layout
    rows_total, d = u_tb.shape
    nch, uc, sp = bre.shape
    rows = rows_total // n_tc
    assert rows % 8 == 0 and nb in (4, 8)
    kern = functools.partial(_s5_kernel, nb=nb, n_tc=n_tc, sp=sp)
    par = lambda r, c: pl.BlockSpec((None, r, c), lambda g, t: (g, 0, 0))
    st_spec = pl.BlockSpec((8, sp), lambda g, t: (0, g))
    return pl.pallas_call(
        kern,
        grid=(nch, n_tc),
        in_specs=[pl.BlockSpec((rows, uc), lambda g, t: (t, g)),
                  par(1, sp), par(1, sp), par(1, sp),
                  par(uc, sp), par(uc, sp), par(sp, uc), par(sp, uc),
                  pl.BlockSpec((1, uc), lambda g, t: (0, g)),
                  st_spec, st_spec],
        out_specs=[pl.BlockSpec((rows, uc), lambda g, t: (t, g)), st_spec, st_spec],
        out_shape=[jax.ShapeDtypeStruct((rows_total, d), BF16),
                   jax.ShapeDtypeStruct(x0_re.shape, F32), jax.ShapeDtypeStruct(x0_re.shape, F32)],
        scratch_shapes=[pltpu.VMEM((uc, 2 * sp), BF16), pltpu.VMEM((2 * sp, uc), BF16),
                        pltpu.VMEM((1, sp), F32), pltpu.VMEM((1, sp), F32),
                        pltpu.VMEM((rows, 2 * sp), F32),
                        pltpu.VMEM((8, sp), F32), pltpu.VMEM((8, sp), F32)],
        compiler_params=_params(48, ("parallel", "arbitrary")),
        name="s5_scan",
    )(u_tb, lre, lim, ldt, bre, bim, cre, cim, d_skip.reshape(1, d), x0_re, x0_im)


def _row_tile(m, cap):
    best = None
    for t in range(16, cap + 1, 16):
        if m % t == 0:
            best = t
    return best if best is not None else m


def _col_tile(n, cap):
    best = LANES
    for t in range(LANES, cap + 1, LANES):
        if n % t == 0:
            best = t
    return best


def _time_chunks(t_len, nb, cap_rows):
    best = 1
    for n in range(1, t_len + 1):
        if t_len % n:
            continue
        steps = t_len // n
        rows = steps * nb
        if rows % 16 == 0 and rows <= cap_rows:
            return n
        best = n
    return best


def kernel(x_prompt, x_sample, cache_k, cache_v, cache_logf, state_s5_re, state_s5_im, page_table,
           meta_tokens, norm_g, ffn_w_up, ffn_w_down, attn_w_in, attn_b_f, attn_w_out,
           s5_A_re, s5_A_im, s5_B_re, s5_B_im, s5_C_re, s5_C_im, s5_log_dt, s5_D, s5_w_glu):
    bsz, seq, d = x_prompt.shape
    db, ds, _ = x_sample.shape
    n_meta = meta_tokens.shape[0]
    t_len = n_meta + seq
    mp, ms = bsz * t_len, db * ds
    m = mp + ms
    depth = norm_g.shape[0]
    n_heads = attn_b_f.shape[-1]
    n_groups, state_dim = s5_A_re.shape[1:]

    tm_small = _row_tile(m, 640)
    tm_big = _row_tile(m, 832)
    up_chunk = _row_tile(m, 640)
    n_tc = _time_chunks(t_len, bsz, 1400)
    tn_qkv = _col_tile(3 * d, 512)
    tn_out = _col_tile(d, 512)
    tq = 256 if seq % 256 == 0 else LANES
    pages_per_step = 4 if page_table.shape[1] % 4 == 0 else 1

    w_down = ffn_w_down.astype(BF16)
    w_in = attn_w_in.astype(BF16)
    w_out = attn_w_out.astype(BF16)
    w_glu = s5_w_glu.astype(BF16)

    meta = jnp.broadcast_to(meta_tokens[None].astype(x_prompt.dtype), (bsz, n_meta, d))
    hp = jnp.concatenate([meta, x_prompt], axis=1).reshape(mp, d)
    h = jnp.concatenate([hp, x_sample.reshape(ms, d)], axis=0)

    def to_time_major(x):
        xp = x[:mp].reshape(bsz, t_len, d).transpose(1, 0, 2).reshape(mp, d)
        xs = x[mp:].reshape(db, ds, d).transpose(1, 0, 2).reshape(ms, d)
        return xp, xs

    def to_batch_major(xp, xs):
        xp = xp.reshape(t_len, bsz, d).transpose(1, 0, 2).reshape(mp, d)
        xs = xs.reshape(ds, db, d).transpose(1, 0, 2).reshape(ms, d)
        return jnp.concatenate([xp, xs], axis=0)

    k_pr, v_pr, lf_pr, sre_pr, sim_pr = [], [], [], [], []
    k_sa, v_sa, lf_sa, sre_sa, sim_sa = [], [], [], [], []
    u = _rmsnorm(h, norm_g[0, 0], tm_big)
    for i in range(depth):
        g = norm_g[i]
        attn_layer = i % 2 == 0
        act = _ffn_up(u, ffn_w_up, i, 0, up_chunk)
        h, u = _ffn_down(act, w_down, i, 0, h, g[1], g[2], BF16 if attn_layer else F32, tm_small)
        if attn_layer:
            li = i // 2
            qkv, logf = _qkv_proj(u, w_in, attn_b_f, li, tm_big, tn_qkv)
            cparts = _prompt_cumsum(logf, bsz, t_len)
            yp = _attn_prompt(qkv, cparts, bsz, t_len, n_meta, n_heads, tq)
            ys = _attn_decode(qkv[mp:], logf[mp:], cache_k, cache_v, cache_logf, page_table, li,
                              pages_per_step)
            y = jnp.concatenate([yp, ys.astype(BF16)], axis=0)
            h, u = _mixer_out(y, w_out, li, False, h, g[3], g[4], tm_small, tn_out)
            k_pr.append(qkv[:mp, d:2 * d]); v_pr.append(qkv[:mp, 2 * d:]); lf_pr.append(logf[:mp])
            k_sa.append(qkv[mp:, d:2 * d]); v_sa.append(qkv[mp:, 2 * d:]); lf_sa.append(logf[mp:])
        else:
            si = i // 2
            layout = _s5_layout(s5_A_re[si], s5_A_im[si], s5_B_re[si], s5_B_im[si],
                                s5_C_re[si], s5_C_im[si], s5_log_dt[si], 16)
            up_tb, us_tb = to_time_major(u)
            zeros = jnp.zeros((8, n_groups * state_dim), F32)
            yp, re_p, im_p = _s5_scan(up_tb, layout, s5_D[si], zeros, zeros, bsz, n_tc)
            ys, re_s, im_s = _s5_scan(us_tb, layout, s5_D[si],
                                      state_s5_re[si].reshape(db, -1), state_s5_im[si].reshape(db, -1),
                                      db, 1)
            y = to_batch_major(yp, ys)
            h, u = _mixer_out(y, w_glu, si, True, h, g[3], g[4], tm_small, tn_out)
            sre_pr.append(re_p[:bsz]); sim_pr.append(im_p[:bsz])
            sre_sa.append(re_s[:db]); sim_sa.append(im_s[:db])
        act = _ffn_up(u, ffn_w_up, i, 1, up_chunk)
        g_next = norm_g[i + 1, 0] if i + 1 < depth else None
        h, u = _ffn_down(act, w_down, i, 1, h, g[5], g_next, BF16, tm_small)

    hd = d // n_heads
    stack = lambda xs, shape: jnp.stack(xs).reshape((len(xs),) + shape)
    return (h[:mp].reshape(bsz, t_len, d)[:, n_meta:], h[mp:].reshape(db, ds, d),
            stack(k_pr, (bsz, t_len, n_heads, hd)), stack(v_pr, (bsz, t_len, n_heads, hd)),
            stack(lf_pr, (bsz, t_len, n_heads)),
            stack(sre_pr, (bsz, n_groups, state_dim)), stack(sim_pr, (bsz, n_groups, state_dim)),
            stack(k_sa, (db, ds, n_heads, hd)), stack(v_sa, (db, ds, n_heads, hd)),
            stack(lf_sa, (db, ds, n_heads)),
            stack(sre_sa, (db, n_groups, state_dim)), stack(sim_sa, (db, n_groups, state_dim)))
```

```python
import functools

import jax
import jax.numpy as jnp
from jax import lax
from jax.experimental import pallas as pl
from jax.experimental.pallas import tpu as pltpu

F32 = jnp.float32
BF16 = jnp.bfloat16
RMS_EPS = 1e-6
LANES = 128
MIB = 1024 * 1024
NT_DIMS = (((1,), (1,)), ((), ()))


def _params(vmem_mib, semantics=None):
    return pltpu.CompilerParams(dimension_semantics=semantics, vmem_limit_bytes=vmem_mib * MIB)


def _rms(x, g):
    ms = jnp.mean(x * x, axis=-1, keepdims=True)
    return (x * lax.rsqrt(ms + RMS_EPS)) * g


def _split3(x):
    hi = x.astype(BF16)
    r1 = x - hi.astype(F32)
    mid = r1.astype(BF16)
    lo = (r1 - mid.astype(F32)).astype(BF16)
    return hi, mid, lo


def _dot(a, b):
    return jnp.dot(a, b, preferred_element_type=F32)


def _dot_nt(a, b):
    return lax.dot_general(a, b, NT_DIMS, preferred_element_type=F32)


def _rmsnorm_kernel(h_ref, g_ref, o_ref):
    o_ref[...] = _rms(h_ref[...], g_ref[...]).astype(o_ref.dtype)


def _rmsnorm(h, g, tm):
    m, d = h.shape
    return pl.pallas_call(
        _rmsnorm_kernel,
        grid=(m // tm,),
        in_specs=[pl.BlockSpec((tm, d), lambda i: (i, 0)), pl.BlockSpec((1, d), lambda i: (0, 0))],
        out_specs=pl.BlockSpec((tm, d), lambda i: (i, 0)),
        out_shape=jax.ShapeDtypeStruct((m, d), BF16),
        compiler_params=_params(40),
        name="rmsnorm",
    )(h, g.reshape(1, d))


def _ffn_up_kernel(up_ref, us_ref, wa_ref, wb_ref, op_ref, os_ref, w_scr, *, n_chunks, ch):
    w_scr[:, :LANES] = wa_ref[...].astype(BF16)
    w_scr[:, LANES:] = wb_ref[...].astype(BF16)

    def swiglu(x):
        z = _dot(x, w_scr[...])
        a = z[:, :LANES]
        b = z[:, LANES:]
        return (a * jax.nn.sigmoid(a) * b).astype(BF16)

    def body(c, carry):
        rows = pl.ds(pl.multiple_of(c * ch, ch), ch)
        op_ref[rows, :] = swiglu(up_ref[rows, :])
        return carry

    lax.fori_loop(0, n_chunks, body, 0, unroll=True)
    os_ref[...] = swiglu(us_ref[...])


def _ffn_up(up, us, w_up, layer, half, ch):
    mp, d = up.shape
    ms = us.shape[0]
    f = w_up.shape[-1] // 2
    nt = f // LANES
    kern = functools.partial(_ffn_up_kernel, n_chunks=mp // ch, ch=ch)
    return pl.pallas_call(
        kern,
        grid=(nt,),
        in_specs=[
            pl.BlockSpec(memory_space=pltpu.VMEM),
            pl.BlockSpec(memory_space=pltpu.VMEM),
            pl.BlockSpec((None, None, d, LANES), lambda j: (layer, half, 0, j)),
            pl.BlockSpec((None, None, d, LANES), lambda j: (layer, half, 0, j + nt)),
        ],
        out_specs=[pl.BlockSpec((mp, LANES), lambda j: (0, j)),
                   pl.BlockSpec((ms, LANES), lambda j: (0, j))],
        out_shape=[jax.ShapeDtypeStruct((mp, f), BF16), jax.ShapeDtypeStruct((ms, f), BF16)],
        scratch_shapes=[pltpu.VMEM((d, 2 * LANES), BF16)],
        compiler_params=_params(56),
        name="ffn_up",
    )(up, us, w_up, w_up)


def _residual_epilogue(y, h_ref, gp_ref, gn_ref, hn_ref, un_ref, scale):
    hn = h_ref[...] + scale * _rms(y, gp_ref[...])
    hn_ref[...] = hn
    if un_ref is not None:
        un_ref[...] = _rms(hn, gn_ref[...]).astype(un_ref.dtype)


def _down_tiling(f):
    nt = f // LANES
    for dmul in (8, 6, 4, 2, 7, 5, 3):
        if nt % dmul == 0 and nt // dmul >= 1:
            return dmul * LANES, nt // dmul, False
    for dmul in (8, 6, 4, 2, 7, 5, 3):
        if (nt - 1) % dmul == 0:
            return dmul * LANES, (nt - 1) // dmul, True
    return LANES, nt, False


class _RowRefs:
    def __init__(self, x, xt, h, hn, un, acc):
        self.x, self.xt, self.h, self.hn, self.un, self.acc = x, xt, h, hn, un, acc


def _unpack_rows(refs, has_tail, has_next):
    refs = list(refs)
    take = lambda cond=True: refs.pop(0) if cond else None
    xp, xtp, xs, xts = take(), take(has_tail), take(), take(has_tail)
    return refs, xp, xtp, xs, xts


def _ffn_down_kernel(*refs, nk, scale, has_tail, has_next):
    refs, xp, xtp, xs, xts = _unpack_rows(refs, has_tail, has_next)
    take = lambda cond=True: refs.pop(0) if cond else None
    wm_ref, wt_ref = take(), take(has_tail)
    hp, hs, gp_ref, gn_ref = take(), take(), take(), take(has_next)
    hnp, unp, hns, uns = take(), take(has_next), take(), take(has_next)
    accp, accs = take(), take()
    prompt = _RowRefs(xp, xtp, hp, hnp, unp, accp)
    sample = _RowRefs(xs, xts, hs, hns, uns, accs)
    i = pl.program_id(0)
    k = pl.program_id(1)

    def accumulate(r):
        @pl.when(k == 0)
        def _():
            if has_tail:
                r.acc[...] = _dot(r.xt[...], wt_ref[...])
            else:
                r.acc[...] = jnp.zeros(r.acc.shape, F32)

        r.acc[...] += _dot(r.x[...], wm_ref[...])

        @pl.when(k == nk - 1)
        def _():
            _residual_epilogue(r.acc[...], r.h, gp_ref, gn_ref, r.hn, r.un, scale)

    accumulate(prompt)
    pl.when(i == 0)(lambda: accumulate(sample))


def _tile_maps(tm, t_len, time_major):
    if not time_major:
        return lambda i, k: (i, 0)
    per_seq = t_len // tm
    return lambda i, k: (i % per_seq, i // per_seq)


def _ffn_down(act_p, act_s, w_down, layer, half, h_p, h_s, g_post, g_next, u_dtype, tm, t_len,
              u_time_major):
    mp, f = act_p.shape
    ms = act_s.shape[0]
    d = h_p.shape[1]
    tk, nk, has_tail = _down_tiling(f)
    has_next = g_next is not None
    tail_blk = (f - LANES) // LANES
    in_specs = [pl.BlockSpec((tm, tk), lambda i, k: (i, k))]
    args = [act_p]
    if has_tail:
        in_specs.append(pl.BlockSpec((tm, LANES), lambda i, k: (i, tail_blk)))
        args.append(act_p)
    in_specs.append(pl.BlockSpec((ms, tk), lambda i, k: (0, k)))
    args.append(act_s)
    if has_tail:
        in_specs.append(pl.BlockSpec((ms, LANES), lambda i, k: (0, tail_blk)))
        args.append(act_s)
    in_specs.append(pl.BlockSpec((None, None, tk, d), lambda i, k: (layer, half, k, 0)))
    args.append(w_down)
    if has_tail:
        in_specs.append(pl.BlockSpec((None, None, LANES, d), lambda i, k: (layer, half, tail_blk, 0)))
        args.append(w_down)
    row_spec = pl.BlockSpec((tm, d), lambda i, k: (i, 0))
    srow_spec = pl.BlockSpec((ms, d), lambda i, k: (0, 0))
    vec_spec = pl.BlockSpec((1, d), lambda i, k: (0, 0))
    in_specs += [row_spec, srow_spec, vec_spec]
    args += [h_p, h_s, g_post.reshape(1, d)]
    out_specs = [row_spec]
    out_shape = [jax.ShapeDtypeStruct((mp, d), F32)]
    if has_next:
        in_specs.append(vec_spec)
        args.append(g_next.reshape(1, d))
        out_specs.append(pl.BlockSpec((tm, d), _tile_maps(tm, t_len, u_time_major)))
        u_shape = (t_len, (mp // t_len) * d) if u_time_major else (mp, d)
        out_shape.append(jax.ShapeDtypeStruct(u_shape, u_dtype))
    out_specs.append(srow_spec)
    out_shape.append(jax.ShapeDtypeStruct((ms, d), F32))
    if has_next:
        out_specs.append(srow_spec)
        out_shape.append(jax.ShapeDtypeStruct((ms, d), u_dtype))
    kern = functools.partial(_ffn_down_kernel, nk=nk, scale=0.5, has_tail=has_tail, has_next=has_next)
    res = pl.pallas_call(
        kern,
        grid=(mp // tm, nk),
        in_specs=in_specs,
        out_specs=out_specs,
        out_shape=out_shape,
        scratch_shapes=[pltpu.VMEM((tm, d), F32), pltpu.VMEM((ms, d), F32)],
        compiler_params=_params(56, ("arbitrary", "arbitrary")),
        name="ffn_down",
    )(*args)
    if has_next:
        return res[0], res[1], res[2], res[3]
    return res[0], None, res[1], None


def _mixer_out_kernel(xp, xs, *refs, nn, tn, glu):
    refs = list(refs)
    take = lambda cond=True: refs.pop(0) if cond else None
    wa_ref, wb_ref = take(), take(glu)
    hp, hs, gp_ref, gn_ref = take(), take(), take(), take()
    hnp, unp, hns, uns = take(), take(), take(), take()
    accp, accs = take(), take()
    prompt = _RowRefs(xp, None, hp, hnp, unp, accp)
    sample = _RowRefs(xs, None, hs, hns, uns, accs)
    i = pl.program_id(0)
    j = pl.program_id(1)
    col = pl.multiple_of(j * tn, tn)

    def project(r):
        x = r.x[...].astype(BF16)
        if glu:
            y = _dot(x, wa_ref[...]) * jax.nn.sigmoid(_dot(x, wb_ref[...]))
        else:
            y = _dot(x, wa_ref[...])
        r.acc[:, pl.ds(col, tn)] = y

        @pl.when(j == nn - 1)
        def _():
            _residual_epilogue(r.acc[...], r.h, gp_ref, gn_ref, r.hn, r.un, 1.0)

    project(prompt)
    pl.when(i == 0)(lambda: project(sample))


def _mixer_out(x_p, x_s, w, layer, glu, h_p, h_s, g_post, g_next, tm, tn, t_len, x_time_major):
    mp, d = h_p.shape
    ms = h_s.shape[0]
    kdim = x_s.shape[1]
    nn = d // tn
    row_spec = pl.BlockSpec((tm, d), lambda i, j: (i, 0))
    srow_spec = pl.BlockSpec((ms, d), lambda i, j: (0, 0))
    vec_spec = pl.BlockSpec((1, d), lambda i, j: (0, 0))
    in_specs = [pl.BlockSpec((tm, kdim), _tile_maps(tm, t_len, x_time_major)),
                pl.BlockSpec((ms, kdim), lambda i, j: (0, 0)),
                pl.BlockSpec((None, kdim, tn), lambda i, j: (layer, 0, j))]
    args = [x_p, x_s, w]
    if glu:
        in_specs.append(pl.BlockSpec((None, kdim, tn), lambda i, j: (layer, 0, j + nn)))
        args.append(w)
    in_specs += [row_spec, srow_spec, vec_spec, vec_spec]
    args += [h_p, h_s, g_post.reshape(1, d), g_next.reshape(1, d)]
    return pl.pallas_call(
        functools.partial(_mixer_out_kernel, nn=nn, tn=tn, glu=glu),
        grid=(mp // tm, nn),
        in_specs=in_specs,
        out_specs=[row_spec, row_spec, srow_spec, srow_spec],
        out_shape=[jax.ShapeDtypeStruct((mp, d), F32), jax.ShapeDtypeStruct((mp, d), BF16),
                   jax.ShapeDtypeStruct((ms, d), F32), jax.ShapeDtypeStruct((ms, d), BF16)],
        scratch_shapes=[pltpu.VMEM((tm, d), F32), pltpu.VMEM((ms, d), F32)],
        compiler_params=_params(56, ("arbitrary", "arbitrary")),
        name="mixer_out_glu" if glu else "mixer_out",
    )(*args)


def _log_sigmoid(x):
    return jnp.minimum(x, 0.0) - jnp.log1p(jnp.exp(-jnp.abs(x)))


def _qkv_kernel(x_ref, w_ref, wf_ref, bf_ref, o_ref, lf_ref, *, n_heads):
    j = pl.program_id(1)
    x = x_ref[...]
    o_ref[...] = _dot(x, w_ref[...])

    @pl.when(j == 0)
    def _():
        f = _dot(x, wf_ref[...])[:, :n_heads] + bf_ref[...]
        lf_ref[...] = _log_sigmoid(f)


def _qkv_proj(u, w_in, b_f, layer, tm, tn):
    m, d = u.shape
    n_heads = b_f.shape[-1]
    n_main = 3 * d
    return pl.pallas_call(
        functools.partial(_qkv_kernel, n_heads=n_heads),
        grid=(m // tm, n_main // tn),
        in_specs=[
            pl.BlockSpec((tm, d), lambda i, j: (i, 0)),
            pl.BlockSpec((None, d, tn), lambda i, j: (layer, 0, j)),
            pl.BlockSpec((None, d, LANES), lambda i, j: (layer, 0, n_main // LANES)),
            pl.BlockSpec((None, 1, n_heads), lambda i, j: (layer, 0, 0)),
        ],
        out_specs=[pl.BlockSpec((tm, tn), lambda i, j: (i, j)),
                   pl.BlockSpec((tm, n_heads), lambda i, j: (i, 0))],
        out_shape=[jax.ShapeDtypeStruct((m, n_main), F32), jax.ShapeDtypeStruct((m, n_heads), F32)],
        compiler_params=_params(48, ("parallel", "arbitrary")),
        name="qkv_proj",
    )(u, w_in, w_in, b_f.reshape(b_f.shape[0], 1, n_heads))


def _cumsum_kernel(lf_ref, o_ref, *, t_len, chunk, n_heads):
    r = lax.broadcasted_iota(jnp.int32, (chunk, chunk), 0)
    c = lax.broadcasted_iota(jnp.int32, (chunk, chunk), 1)
    tri = jnp.where(c <= r, 1.0, 0.0).astype(BF16)
    carry = jnp.zeros((1, n_heads), F32)
    for s in range(t_len // chunk):
        x = lf_ref[s * chunk:(s + 1) * chunk, :]
        hi, mid, lo = _split3(x)
        cs = _dot(tri, hi) + _dot(tri, mid) + _dot(tri, lo) + carry
        carry = cs[chunk - 1:chunk, :]
        chi, cmid, clo = _split3(cs)
        o_ref[s * chunk:(s + 1) * chunk, 0:n_heads] = chi
        o_ref[s * chunk:(s + 1) * chunk, n_heads:2 * n_heads] = cmid
        o_ref[s * chunk:(s + 1) * chunk, 2 * n_heads:3 * n_heads] = clo


def _cumsum_chunk(t_len):
    for c in (688, 256, 128, 64, 48, 16):
        if t_len % c == 0:
            return c
    return t_len


def _prompt_cumsum(logf, batch, t_len):
    n_heads = logf.shape[1]
    kern = functools.partial(_cumsum_kernel, t_len=t_len, chunk=_cumsum_chunk(t_len), n_heads=n_heads)
    return pl.pallas_call(
        kern,
        grid=(batch,),
        in_specs=[pl.BlockSpec((t_len, n_heads), lambda b: (b, 0))],
        out_specs=pl.BlockSpec((t_len, 3 * n_heads), lambda b: (b, 0)),
        out_shape=jax.ShapeDtypeStruct((batch * t_len, 3 * n_heads), BF16),
        compiler_params=_params(32, ("parallel",)),
        name="prompt_cumsum",
    )(logf)


def _store_transposed(src_ref, dst_ref, t_len):
    n_full = t_len // LANES
    for c in range(n_full):
        dst_ref[:, c * LANES:(c + 1) * LANES] = src_ref[c * LANES:(c + 1) * LANES, :].T
    rem = t_len - n_full * LANES
    if rem:
        last = src_ref[t_len - LANES:t_len, :].T
        dst_ref[:, n_full * LANES:t_len] = last[:, LANES - rem:]


def _attn_prompt_kernel(q_ref, k_ref, v_ref, cp_ref, *rest, t_len, n_meta, tq, n_heads, head_dim,
                        aliased):
    if aliased:
        rest = rest[2:]
    o_ref, kt_ref, vt_ref, qa_scr, ka_scr, v_scr = rest
    _store_transposed(k_ref, kt_ref, t_len)
    _store_transposed(v_ref, vt_ref, t_len)
    hp = pl.program_id(1)
    half = LANES // 2
    lane = lax.broadcasted_iota(jnp.int32, (1, LANES), 1)
    r3 = lax.broadcasted_iota(jnp.int32, (3 * n_heads, LANES), 0)
    l3 = lax.broadcasted_iota(jnp.int32, (3 * n_heads, LANES), 1)
    cp = cp_ref[...]
    q = (q_ref[...] * (head_dim ** -0.5)).astype(BF16)
    k = k_ref[...].astype(BF16)
    v_scr[...] = v_ref[...].astype(BF16)
    for hh in range(2):
        head = 2 * hp + hh
        base = half * (1 - hh)
        sel_q = jnp.zeros((3 * n_heads, LANES), F32)
        sel_k = jnp.zeros((3 * n_heads, LANES), F32)
        for part in range(3):
            row_hit = r3 == head + part * n_heads
            sel_q = sel_q + jnp.where(row_hit & (l3 == base + part), 1.0, 0.0)
            sel_k = sel_k + jnp.where(row_hit & (l3 == base + 3 + part), -1.0, 0.0)
        ones_q = jnp.where((lane >= base + 3) & (lane < base + 6), 1.0, 0.0)
        ones_k = jnp.where((lane >= base) & (lane < base + 3), 1.0, 0.0)
        ex_q = (_dot(cp, sel_q.astype(BF16)) + ones_q).astype(BF16)
        ex_k = (_dot(cp, sel_k.astype(BF16)) + ones_k).astype(BF16)
        data = (lane >= half * hh) & (lane < half * hh + half)
        qa_scr[hh] = jnp.where(data, q, ex_q)
        ka_scr[hh] = jnp.where(data, k, ex_k)

    tiles = [(0, n_meta)] + [(n_meta + i * tq, tq) for i in range((t_len - n_meta) // tq)]
    for q0, tl in tiles:
        row = lax.broadcasted_iota(jnp.int32, (tl, tl), 0)
        col = lax.broadcasted_iota(jnp.int32, (tl, tl), 1)
        causal = col <= row
        outs = []
        for hh in range(2):
            qt = qa_scr[hh, q0:q0 + tl, :]
            sd = jnp.where(causal, _dot_nt(qt, ka_scr[hh, q0:q0 + tl, :]), -jnp.inf)
            m = jnp.max(sd, axis=-1, keepdims=True)
            if q0 > 0:
                so = _dot_nt(qt, ka_scr[hh, 0:q0, :])
                m = jnp.maximum(m, jnp.max(so, axis=-1, keepdims=True))
            pd = jnp.exp(sd - m)
            den = jnp.sum(pd, axis=-1, keepdims=True)
            o = _dot(pd.astype(BF16), v_scr[q0:q0 + tl, :])
            if q0 > 0:
                po = jnp.exp(so - m)
                den = den + jnp.sum(po, axis=-1, keepdims=True)
                o = o + _dot(po.astype(BF16), v_scr[0:q0, :])
            outs.append(o / den)
        o_ref[q0:q0 + tl, :] = jnp.where(lane < half, outs[0], outs[1]).astype(o_ref.dtype)


def _attn_prompt(qkv, cparts, batch, t_len, n_meta, n_heads, tq, layer, n_layers, kv_t):
    d = qkv.shape[1] // 3
    head_dim = d // n_heads
    assert 2 * head_dim == LANES, "head pairs must fill one lane tile"
    assert (t_len - n_meta) % tq == 0 and t_len >= LANES
    npair = n_heads // 2
    aliased = kv_t is not None
    kern = functools.partial(_attn_prompt_kernel, t_len=t_len, n_meta=n_meta, tq=tq,
                             n_heads=n_heads, head_dim=head_dim, aliased=aliased)
    blk = lambda off: pl.BlockSpec((t_len, LANES), lambda b, p: (b, off + p))
    in_specs = [blk(0), blk(npair), blk(2 * npair),
                pl.BlockSpec((t_len, 3 * n_heads), lambda b, p: (b, 0))]
    args = [qkv, qkv, qkv, cparts]
    aliases = {}
    if aliased:
        in_specs += [pl.BlockSpec(memory_space=pl.ANY), pl.BlockSpec(memory_space=pl.ANY)]
        args += list(kv_t)
        aliases = {4: 1, 5: 2}
    t_spec = pl.BlockSpec((None, None, LANES, t_len), lambda b, p: (layer, b, p, 0))
    t_shape = jax.ShapeDtypeStruct((n_layers, batch, d, t_len), F32)
    y, k_t, v_t = pl.pallas_call(
        kern,
        grid=(batch, npair),
        in_specs=in_specs,
        out_specs=[pl.BlockSpec((t_len, LANES), lambda b, p: (b, p)), t_spec, t_spec],
        out_shape=[jax.ShapeDtypeStruct((batch * t_len, d), BF16), t_shape, t_shape],
        scratch_shapes=[pltpu.VMEM((2, t_len, LANES), BF16), pltpu.VMEM((2, t_len, LANES), BF16),
                        pltpu.VMEM((t_len, LANES), BF16)],
        input_output_aliases=aliases,
        compiler_params=_params(48, ("parallel", "arbitrary")),
        name="attn_prompt",
    )(*args)
    return y, (k_t, v_t)


def _attn_decode_kernel(pt_ref, q_ref, kn_ref, vn_ref, lfn_ref, *rest,
                        n_pg, n_heads, head_dim, n_new, page, n_steps):
    del pt_ref
    k_refs = rest[:n_pg]
    v_refs = rest[n_pg:2 * n_pg]
    lf_refs = rest[2 * n_pg:3 * n_pg]
    o_ref = rest[3 * n_pg]
    (q_scr, s_scr, p_scr, acc_scr, m_scr, l_scr, cnew_scr, alpha_scr,
     tail_scr, b_scr) = rest[3 * n_pg + 1:]
    step = pl.program_id(1)
    rows = n_heads * n_new
    head_unroll = 8 if n_heads % 8 == 0 else 1
    er = lax.broadcasted_iota(jnp.int32, (rows, n_heads), 0)
    ec = lax.broadcasted_iota(jnp.int32, (rows, n_heads), 1)
    expand = jnp.where((er >= ec * n_new) & (er < (ec + 1) * n_new), 1.0, 0.0).astype(BF16)

    def spread(lf):
        return [_dot_nt(expand, part).astype(BF16) for part in _split3(lf)]

    def head_update(h8, p_h, v_h, first):
        contrib = _dot(p_h, v_h)
        if first:
            acc_scr[h8, :] = contrib
        else:
            acc_scr[h8, :] = alpha_scr[h8, 0:head_dim] * acc_scr[h8, :] + contrib

    @pl.when(step == 0)
    def _():
        scale = head_dim ** -0.5
        for h in range(n_heads):
            q_scr[h * n_new:(h + 1) * n_new, :] = q_ref[:, h * head_dim:(h + 1) * head_dim] * scale
        kr = lax.broadcasted_iota(jnp.int32, (n_new, LANES), 0)
        kc = lax.broadcasted_iota(jnp.int32, (n_new, LANES), 1)
        upper = jnp.where(kr <= kc, 1.0, 0.0).astype(BF16)
        parts = spread(lfn_ref[...])
        cum = _dot(parts[0], upper) + _dot(parts[1], upper) + _dot(parts[2], upper)
        rq = lax.broadcasted_iota(jnp.int32, (rows, LANES), 0)
        lq = lax.broadcasted_iota(jnp.int32, (rows, LANES), 1)
        qpos = rq & (n_new - 1)
        c_q = jnp.sum(jnp.where(lq == qpos, cum, 0.0), axis=-1, keepdims=True)
        cnew_scr[...] = jnp.broadcast_to(c_q, (rows, LANES))
        bias = c_q - cum
        valid = lq <= qpos
        s_scr[:, 0:LANES] = jnp.zeros((rows, LANES), F32)
        for h in range(n_heads):
            h8 = pl.ds(h * n_new, n_new)
            k_h = kn_ref[:, h * head_dim:(h + 1) * head_dim].astype(BF16)
            s_scr[h8, 0:n_new] = _dot_nt(q_scr[h8, :].astype(BF16), k_h)
        s = jnp.where(valid, s_scr[:, 0:LANES] + bias, -jnp.inf)
        m = jnp.max(s, axis=-1, keepdims=True)
        p = jnp.exp(s - m)
        m_scr[...] = jnp.broadcast_to(m, (rows, LANES))
        l_scr[...] = p
        tail_scr[...] = jnp.zeros((n_heads, LANES), F32)
        p_scr[:, 0:LANES] = p
        for h in range(n_heads):
            h8 = pl.ds(h * n_new, n_new)
            v_h = vn_ref[:, h * head_dim:(h + 1) * head_dim].astype(BF16)
            head_update(h8, p_scr[h8, 0:n_new].astype(BF16), v_h, True)

    kr = lax.broadcasted_iota(jnp.int32, (page, 2 * page), 0)
    kc = lax.broadcasted_iota(jnp.int32, (page, 2 * page), 1)
    after = jnp.where((kr > kc) | (kc >= page), 1.0, 0.0).astype(BF16)

    tail = tail_scr[...]
    for j in range(n_pg):
        hi, mid, lo = _split3(lf_refs[j][...])
        full = _dot(hi, after) + _dot(mid, after) + _dot(lo, after)
        b_scr[:, j * page:(j + 1) * page] = full[:, 0:page] + tail
        tail = tail + full[:, page:2 * page]
    tail_scr[...] = tail

    def scores(h, carry):
        h8 = pl.ds(pl.multiple_of(h * n_new, n_new), n_new)
        q_h = q_scr[h8, :].astype(BF16)
        k_h = jnp.concatenate([k_refs[j][h] for j in range(n_pg)], axis=1).astype(BF16)
        s_scr[h8, :] = _dot(q_h, k_h) + b_scr[pl.ds(h, 1), :]
        return carry

    lax.fori_loop(0, n_heads, scores, 0, unroll=head_unroll)

    cnew = cnew_scr[...]
    s_all = [s_scr[:, j * page:(j + 1) * page] + cnew for j in range(n_pg)]
    m_chunk = s_all[0]
    for j in range(1, n_pg):
        m_chunk = jnp.maximum(m_chunk, s_all[j])
    m_old = m_scr[...]
    m_new = jnp.maximum(m_old, jnp.max(m_chunk, axis=-1, keepdims=True))
    alpha = jnp.exp(m_old - m_new)
    m_scr[...] = m_new
    alpha_scr[...] = alpha
    l_new = alpha * l_scr[...]
    for j in range(n_pg):
        p = jnp.exp(s_all[j] - m_new)
        l_new = l_new + p
        p_scr[:, j * page:(j + 1) * page] = p
    l_scr[...] = l_new

    def values(h, carry):
        h8 = pl.ds(pl.multiple_of(h * n_new, n_new), n_new)
        v_h = jnp.concatenate([v_refs[j][h] for j in range(n_pg)], axis=1).astype(BF16)
        contrib = _dot_nt(p_scr[h8, :].astype(BF16), v_h)
        acc_scr[h8, :] = alpha_scr[h8, 0:head_dim] * acc_scr[h8, :] + contrib
        return carry

    lax.fori_loop(0, n_heads, values, 0, unroll=head_unroll)

    @pl.when(step == n_steps - 1)
    def _():
        den = jnp.sum(l_scr[...], axis=-1, keepdims=True)
        res = acc_scr[...] / den
        for h in range(n_heads):
            o_ref[:, h * head_dim:(h + 1) * head_dim] = res[h * n_new:(h + 1) * n_new, :]


def _attn_decode(qkv_s, logf_s, cache_k, cache_v, cache_logf, page_table, layer, n_pg):
    n_layers, n_pool, page, n_heads, head_dim = cache_k.shape
    db, n_pages = page_table.shape
    n_new = qkv_s.shape[0] // db
    d = n_heads * head_dim
    rows = n_heads * n_new
    n_steps = n_pages // n_pg
    ck = cache_k.transpose(0, 1, 3, 4, 2)
    cv = cache_v.transpose(0, 1, 3, 4, 2)
    clf = cache_logf.transpose(0, 1, 3, 2)

    def page_of(b, s, pt, j):
        return pt[b * n_pages + (n_pages - 1 - (s * n_pg + j))]

    kv_spec = lambda j: pl.BlockSpec((None, None, n_heads, head_dim, page),
                                     lambda b, s, pt: (layer, page_of(b, s, pt, j), 0, 0, 0))
    lf_spec = lambda j: pl.BlockSpec((None, None, n_heads, page),
                                     lambda b, s, pt: (layer, page_of(b, s, pt, j), 0, 0))
    new_spec = lambda c: pl.BlockSpec((n_new, d), lambda b, s, pt: (b, c))
    in_specs = [new_spec(0), new_spec(1), new_spec(2),
                pl.BlockSpec((n_new, n_heads), lambda b, s, pt: (b, 0))]
    in_specs += [kv_spec(j) for j in range(n_pg)] + [kv_spec(j) for j in range(n_pg)]
    in_specs += [lf_spec(j) for j in range(n_pg)]
    kern = functools.partial(_attn_decode_kernel, n_pg=n_pg, n_heads=n_heads, head_dim=head_dim,
                             n_new=n_new, page=page, n_steps=n_steps)
    stat = pltpu.VMEM((rows, LANES), F32)
    grid_spec = pltpu.PrefetchScalarGridSpec(
        num_scalar_prefetch=1,
        grid=(db, n_steps),
        in_specs=in_specs,
        out_specs=pl.BlockSpec((n_new, d), lambda b, s, pt: (b, 0)),
        scratch_shapes=[pltpu.VMEM((rows, head_dim), F32),
                        pltpu.VMEM((rows, n_pg * page), F32),
                        pltpu.VMEM((rows, n_pg * page), F32),
                        pltpu.VMEM((rows, head_dim), F32),
                        stat, stat, stat, stat,
                        pltpu.VMEM((n_heads, LANES), F32),
                        pltpu.VMEM((n_heads, n_pg * page), F32)],
    )
    return pl.pallas_call(
        kern,
        grid_spec=grid_spec,
        out_shape=jax.ShapeDtypeStruct((db * n_new, d), F32),
        compiler_params=_params(48, ("parallel", "arbitrary")),
        name="attn_decode",
    )(page_table.reshape(-1), qkv_s, qkv_s, qkv_s, logf_s,
      *([ck] * n_pg), *([cv] * n_pg), *([clf] * n_pg))


def _s5_kernel(u_ref, lre_ref, lim_ref, ldt_ref, bre_ref, bim_ref, cre_ref, cim_ref, d_ref,
               x0r_ref, x0i_ref, y_ref, fr_ref, fi_ref,
               w_scr, c_scr, ar_scr, ai_scr, xs_scr, sr_scr, si_scr, *, nb, n_tc, sp):
    tc = pl.program_id(1)
    rows = xs_scr.shape[0]

    @pl.when(tc == 0)
    def _():
        lre = lre_ref[...]
        lim = lim_ref[...]
        dt = jnp.exp(ldt_ref[...])
        mag = jnp.exp(lre * dt)
        ar = mag * jnp.cos(lim * dt)
        ai = mag * jnp.sin(lim * dt)
        ar_scr[...] = ar
        ai_scr[...] = ai
        xr = ar - 1.0
        den = lre * lre + lim * lim
        cr = (xr * lre + ai * lim) / den
        ci = (ai * lre - xr * lim) / den
        bre = bre_ref[...]
        bim = bim_ref[...]
        w_scr[:, 0:sp] = (bre * cr - bim * ci).astype(BF16)
        w_scr[:, sp:2 * sp] = (bre * ci + bim * cr).astype(BF16)
        c_scr[0:sp, :] = cre_ref[...].astype(BF16)
        c_scr[sp:2 * sp, :] = (-cim_ref[...]).astype(BF16)
        sr_scr[...] = x0r_ref[...]
        si_scr[...] = x0i_ref[...]

    u = u_ref[...]
    xs_scr[...] = _dot(u.astype(BF16), w_scr[...])
    ar8 = jnp.broadcast_to(ar_scr[...], (8, sp))
    ai8 = jnp.broadcast_to(ai_scr[...], (8, sp))
    low = lax.broadcasted_iota(jnp.int32, (8, sp), 0) < 4

    def body(s, carry):
        pr, pi = carry
        r0 = pl.ds(pl.multiple_of(s * 8, 8), 8)
        br = xs_scr[r0, 0:sp]
        bi = xs_scr[r0, sp:2 * sp]
        er = ar8 * pr - ai8 * pi + br
        ei = ar8 * pi + ai8 * pr + bi
        if nb == 8:
            xs_scr[r0, 0:sp] = er
            xs_scr[r0, sp:2 * sp] = ei
            return er, ei
        er4 = pltpu.roll(er, 4, 0)
        ei4 = pltpu.roll(ei, 4, 0)
        orr = ar8 * er4 - ai8 * ei4 + br
        oi = ar8 * ei4 + ai8 * er4 + bi
        xs_scr[r0, 0:sp] = jnp.where(low, er, orr)
        xs_scr[r0, sp:2 * sp] = jnp.where(low, ei, oi)
        return pltpu.roll(orr, 4, 0), pltpu.roll(oi, 4, 0)

    fr, fi = lax.fori_loop(0, rows // 8, body, (sr_scr[...], si_scr[...]))
    sr_scr[...] = fr
    si_scr[...] = fi
    y = _dot(xs_scr[...].astype(BF16), c_scr[...]) + d_ref[...] * u
    y_ref[...] = jax.nn.gelu(y).astype(y_ref.dtype)

    @pl.when(tc == n_tc - 1)
    def _():
        fr_ref[...] = fr
        fi_ref[...] = fi


def _s5_layout(a_re, a_im, b_re, b_im, c_re, c_im, log_dt, gc):
    g, p = a_re.shape
    c = b_re.shape[-1]
    nch = g // gc
    eye = jnp.eye(gc, dtype=F32)

    def bd_in(b):
        b4 = b.reshape(nch, gc, p, c).transpose(0, 1, 3, 2)
        return (b4[:, :, :, None, :] * eye[None, :, None, :, None]).reshape(nch, gc * c, gc * p)

    def bd_out(cm):
        c4 = cm.reshape(nch, gc, c, p).transpose(0, 1, 3, 2)
        return (c4[:, :, :, None, :] * eye[None, :, None, :, None]).reshape(nch, gc * p, gc * c)

    row = lambda x: x.reshape(nch, 1, gc * p)
    return (row(a_re), row(a_im), row(jnp.repeat(log_dt, p)),
            bd_in(b_re), bd_in(b_im), bd_out(c_re), bd_out(c_im))


def _s5_scan(u_tb, layout, d_skip, x0_re, x0_im, nb, n_tc):
    lre, lim, ldt, bre, bim, cre, cim = layout
    rows_total, d = u_tb.shape
    nch, uc, sp = bre.shape
    rows = rows_total // n_tc
    assert rows % 8 == 0 and nb in (4, 8)
    kern = functools.partial(_s5_kernel, nb=nb, n_tc=n_tc, sp=sp)
    par = lambda r, c: pl.BlockSpec((None, r, c), lambda g, t: (g, 0, 0))
    st_spec = pl.BlockSpec((8, sp), lambda g, t: (0, g))
    return pl.pallas_call(
        kern,
        grid=(nch, n_tc),
        in_specs=[pl.BlockSpec((rows, uc), lambda g, t: (t, g)),
                  par(1, sp), par(1, sp), par(1, sp),
                  par(uc, sp), par(uc, sp), par(sp, uc), par(sp, uc),
                  pl.BlockSpec((1, uc), lambda g, t: (0, g)),
                  st_spec, st_spec],
        out_specs=[pl.BlockSpec((rows, uc), lambda g, t: (t, g)), st_spec, st_spec],
        out_shape=[jax.ShapeDtypeStruct((rows_total, d), BF16),
                   jax.ShapeDtypeStruct(x0_re.shape, F32), jax.ShapeDtypeStruct(x0_re.shape, F32)],
        scratch_shapes=[pltpu.VMEM((uc, 2 * sp), BF16), pltpu.VMEM((2 * sp, uc), BF16),
                        pltpu.VMEM((1, sp), F32), pltpu.VMEM((1, sp), F32),
                        pltpu.VMEM((rows, 2 * sp), F32),
                        pltpu.VMEM((8, sp), F32), pltpu.VMEM((8, sp), F32)],
        compiler_params=_params(48, ("parallel", "arbitrary")),
        name="s5_scan",
    )(u_tb, lre, lim, ldt, bre, bim, cre, cim, d_skip.reshape(1, d), x0_re, x0_im)


def _row_tile(m, cap):
    best = None
    for t in range(16, cap + 1, 16):
        if m % t == 0:
            best = t
    return best if best is not None else m


def _col_tile(n, cap):
    best = LANES
    for t in range(LANES, cap + 1, LANES):
        if n % t == 0:
            best = t
    return best


def _time_chunks(t_len, nb, cap_rows):
    best = 1
    for n in range(1, t_len + 1):
        if t_len % n:
            continue
        steps = t_len // n
        rows = steps * nb
        if rows % 16 == 0 and rows <= cap_rows:
            return n
        best = n
    return best


def kernel(x_prompt, x_sample, cache_k, cache_v, cache_logf, state_s5_re, state_s5_im, page_table,
           meta_tokens, norm_g, ffn_w_up, ffn_w_down, attn_w_in, attn_b_f, attn_w_out,
           s5_A_re, s5_A_im, s5_B_re, s5_B_im, s5_C_re, s5_C_im, s5_log_dt, s5_D, s5_w_glu):
    bsz, seq, d = x_prompt.shape
    db, ds, _ = x_sample.shape
    n_meta = meta_tokens.shape[0]
    t_len = n_meta + seq
    mp, ms = bsz * t_len, db * ds
    depth = norm_g.shape[0]
    n_heads = attn_b_f.shape[-1]
    n_groups, state_dim = s5_A_re.shape[1:]

    tm = _row_tile(t_len, 704)
    n_tc = _time_chunks(t_len, bsz, 1400)
    tn_qkv = _col_tile(3 * d, 512)
    tn_out = _col_tile(d, 512)
    tq = 256 if seq % 256 == 0 else LANES
    pages_per_step = 4 if page_table.shape[1] % 4 == 0 else 1
    n_attn = (depth + 1) // 2

    w_down = ffn_w_down.astype(BF16)
    w_in = attn_w_in.astype(BF16)
    w_out = attn_w_out.astype(BF16)
    w_glu = s5_w_glu.astype(BF16)

    meta = jnp.broadcast_to(meta_tokens[None].astype(x_prompt.dtype), (bsz, n_meta, d))
    hp = jnp.concatenate([meta, x_prompt], axis=1).reshape(mp, d)
    hs = x_sample.reshape(ms, d)

    def swap_sample(x, outer, inner):
        return x.reshape(outer, inner, d).transpose(1, 0, 2).reshape(ms, d)

    lf_pr, sre_pr, sim_pr = [], [], []
    k_sa, v_sa, lf_sa, sre_sa, sim_sa = [], [], [], [], []
    kv_t = None
    up = _rmsnorm(hp, norm_g[0, 0], tm)
    us = _rmsnorm(hs, norm_g[0, 0], ms)
    for i in range(depth):
        g = norm_g[i]
        attn_layer = i % 2 == 0
        act_p, act_s = _ffn_up(up, us, ffn_w_up, i, 0, tm)
        hp, up, hs, us = _ffn_down(act_p, act_s, w_down, i, 0, hp, hs, g[1], g[2],
                                   BF16 if attn_layer else F32, tm, t_len, not attn_layer)
        if attn_layer:
            li = i // 2
            qkv_p, logf_p = _qkv_proj(up, w_in, attn_b_f, li, tm, tn_qkv)
            qkv_s, logf_s = _qkv_proj(us, w_in, attn_b_f, li, ms, tn_qkv)
            cparts = _prompt_cumsum(logf_p, bsz, t_len)
            yp, kv_t = _attn_prompt(qkv_p, cparts, bsz, t_len, n_meta, n_heads, tq, li, n_attn, kv_t)
            ys = _attn_decode(qkv_s, logf_s, cache_k, cache_v, cache_logf, page_table, li,
                              pages_per_step)
            hp, up, hs, us = _mixer_out(yp, ys, w_out, li, False, hp, hs, g[3], g[4], tm, tn_out,
                                        t_len, False)
            lf_pr.append(logf_p)
            k_sa.append(qkv_s[:, d:2 * d]); v_sa.append(qkv_s[:, 2 * d:]); lf_sa.append(logf_s)
        else:
            si = i // 2
            layout = _s5_layout(s5_A_re[si], s5_A_im[si], s5_B_re[si], s5_B_im[si],
                                s5_C_re[si], s5_C_im[si], s5_log_dt[si], 16)
            zeros = jnp.zeros((8, n_groups * state_dim), F32)
            yp, re_p, im_p = _s5_scan(up.reshape(mp, d), layout, s5_D[si], zeros, zeros, bsz, n_tc)
            ys, re_s, im_s = _s5_scan(swap_sample(us, db, ds), layout, s5_D[si],
                                      state_s5_re[si].reshape(db, -1), state_s5_im[si].reshape(db, -1),
                                      db, 1)
            hp, up, hs, us = _mixer_out(yp.reshape(t_len, bsz * d), swap_sample(ys, ds, db), w_glu, si,
                                        True, hp, hs, g[3], g[4], tm, tn_out, t_len, True)
            sre_pr.append(re_p[:bsz]); sim_pr.append(im_p[:bsz])
            sre_sa.append(re_s[:db]); sim_sa.append(im_s[:db])
        act_p, act_s = _ffn_up(up, us, ffn_w_up, i, 1, tm)
        g_next = norm_g[i + 1, 0] if i + 1 < depth else None
        hp, up, hs, us = _ffn_down(act_p, act_s, w_down, i, 1, hp, hs, g[5], g_next, BF16, tm, t_len,
                                   False)

    hd = d // n_heads
    stack = lambda xs, shape: jnp.stack(xs).reshape((len(xs),) + shape)
    kv_out = lambda x: x.reshape(n_attn, bsz, n_heads, hd, t_len).transpose(0, 1, 4, 2, 3)
    return (hp.reshape(bsz, t_len, d)[:, n_meta:], hs.reshape(db, ds, d),
            kv_out(kv_t[0]), kv_out(kv_t[1]),
            stack(lf_pr, (bsz, t_len, n_heads)),
            stack(sre_pr, (bsz, n_groups, state_dim)), stack(sim_pr, (bsz, n_groups, state_dim)),
            stack(k_sa, (db, ds, n_heads, hd)), stack(v_sa, (db, ds, n_heads, hd)),
            stack(lf_sa, (db, ds, n_heads)),
            stack(sre_sa, (db, n_groups, state_dim)), stack(sim_sa, (db, n_groups, state_dim)))
```

```python
import functools

import jax
import jax.numpy as jnp
from jax import lax
from jax.experimental import pallas as pl
from jax.experimental.pallas import tpu as pltpu

F32 = jnp.float32
BF16 = jnp.bfloat16
RMS_EPS = 1e-6
LANES = 128
MIB = 1024 * 1024
NT_DIMS = (((1,), (1,)), ((), ()))


def _params(vmem_mib, semantics=None):
    return pltpu.CompilerParams(dimension_semantics=semantics, vmem_limit_bytes=vmem_mib * MIB)


def _rms(x, g):
    ms = jnp.mean(x * x, axis=-1, keepdims=True)
    return (x * lax.rsqrt(ms + RMS_EPS)) * g


def _split3(x):
    hi = x.astype(BF16)
    r1 = x - hi.astype(F32)
    mid = r1.astype(BF16)
    lo = (r1 - mid.astype(F32)).astype(BF16)
    return hi, mid, lo


def _dot(a, b):
    return jnp.dot(a, b, preferred_element_type=F32)


def _dot_nt(a, b):
    return lax.dot_general(a, b, NT_DIMS, preferred_element_type=F32)


def _rmsnorm_kernel(h_ref, g_ref, o_ref):
    o_ref[...] = _rms(h_ref[...], g_ref[...]).astype(o_ref.dtype)


def _rmsnorm(h, g, tm):
    m, d = h.shape
    return pl.pallas_call(
        _rmsnorm_kernel,
        grid=(m // tm,),
        in_specs=[pl.BlockSpec((tm, d), lambda i: (i, 0)), pl.BlockSpec((1, d), lambda i: (0, 0))],
        out_specs=pl.BlockSpec((tm, d), lambda i: (i, 0)),
        out_shape=jax.ShapeDtypeStruct((m, d), BF16),
        compiler_params=_params(40),
        name="rmsnorm",
    )(h, g.reshape(1, d))


def _ffn_up_kernel(up_ref, us_ref, wa_ref, wb_ref, op_ref, os_ref, w_scr, *, n_chunks, ch):
    w_scr[:, :LANES] = wa_ref[...].astype(BF16)
    w_scr[:, LANES:] = wb_ref[...].astype(BF16)

    def swiglu(x):
        z = _dot(x, w_scr[...])
        a = z[:, :LANES]
        b = z[:, LANES:]
        return (a * jax.nn.sigmoid(a) * b).astype(BF16)

    def body(c, carry):
        rows = pl.ds(pl.multiple_of(c * ch, ch), ch)
        op_ref[rows, :] = swiglu(up_ref[rows, :])
        return carry

    lax.fori_loop(0, n_chunks, body, 0, unroll=True)
    os_ref[...] = swiglu(us_ref[...])


def _ffn_up(up, us, w_up, layer, half, ch):
    mp, d = up.shape
    ms = us.shape[0]
    f = w_up.shape[-1] // 2
    nt = f // LANES
    kern = functools.partial(_ffn_up_kernel, n_chunks=mp // ch, ch=ch)
    return pl.pallas_call(
        kern,
        grid=(nt,),
        in_specs=[
            pl.BlockSpec(memory_space=pltpu.VMEM),
            pl.BlockSpec(memory_space=pltpu.VMEM),
            pl.BlockSpec((None, None, d, LANES), lambda j: (layer, half, 0, j)),
            pl.BlockSpec((None, None, d, LANES), lambda j: (layer, half, 0, j + nt)),
        ],
        out_specs=[pl.BlockSpec((mp, LANES), lambda j: (0, j)),
                   pl.BlockSpec((ms, LANES), lambda j: (0, j))],
        out_shape=[jax.ShapeDtypeStruct((mp, f), BF16), jax.ShapeDtypeStruct((ms, f), BF16)],
        scratch_shapes=[pltpu.VMEM((d, 2 * LANES), BF16)],
        compiler_params=_params(56),
        name="ffn_up",
    )(up, us, w_up, w_up)


def _residual_epilogue(y, h_ref, gp_ref, gn_ref, hn_ref, un_ref, scale):
    hn = h_ref[...] + scale * _rms(y, gp_ref[...])
    hn_ref[...] = hn
    if un_ref is not None:
        un_ref[...] = _rms(hn, gn_ref[...]).astype(un_ref.dtype)


def _down_tiling(f):
    nt = f // LANES
    for dmul in (8, 6, 4, 2, 7, 5, 3):
        if nt % dmul == 0 and nt // dmul >= 1:
            return dmul * LANES, nt // dmul, False
    for dmul in (8, 6, 4, 2, 7, 5, 3):
        if (nt - 1) % dmul == 0:
            return dmul * LANES, (nt - 1) // dmul, True
    return LANES, nt, False


class _RowRefs:
    def __init__(self, x, xt, h, hn, un, acc):
        self.x, self.xt, self.h, self.hn, self.un, self.acc = x, xt, h, hn, un, acc


def _unpack_rows(refs, has_tail, has_next):
    refs = list(refs)
    take = lambda cond=True: refs.pop(0) if cond else None
    xp, xtp, xs, xts = take(), take(has_tail), take(), take(has_tail)
    return refs, xp, xtp, xs, xts


def _ffn_down_kernel(*refs, nk, scale, has_tail, has_next):
    refs, xp, xtp, xs, xts = _unpack_rows(refs, has_tail, has_next)
    take = lambda cond=True: refs.pop(0) if cond else None
    wm_ref, wt_ref = take(), take(has_tail)
    hp, hs, gp_ref, gn_ref = take(), take(), take(), take(has_next)
    hnp, unp, hns, uns = take(), take(has_next), take(), take(has_next)
    accp, accs = take(), take()
    prompt = _RowRefs(xp, xtp, hp, hnp, unp, accp)
    sample = _RowRefs(xs, xts, hs, hns, uns, accs)
    i = pl.program_id(0)
    k = pl.program_id(1)

    def accumulate(r):
        @pl.when(k == 0)
        def _():
            if has_tail:
                r.acc[...] = _dot(r.xt[...], wt_ref[...])
            else:
                r.acc[...] = jnp.zeros(r.acc.shape, F32)

        r.acc[...] += _dot(r.x[...], wm_ref[...])

        @pl.when(k == nk - 1)
        def _():
            _residual_epilogue(r.acc[...], r.h, gp_ref, gn_ref, r.hn, r.un, scale)

    accumulate(prompt)
    pl.when(i == 0)(lambda: accumulate(sample))


def _ffn_down(act_p, act_s, w_down, layer, half, h_p, h_s, g_post, g_next, u_dtype, tm):
    mp, f = act_p.shape
    ms = act_s.shape[0]
    d = h_p.shape[1]
    tk, nk, has_tail = _down_tiling(f)
    has_next = g_next is not None
    tail_blk = (f - LANES) // LANES
    in_specs = [pl.BlockSpec((tm, tk), lambda i, k: (i, k))]
    args = [act_p]
    if has_tail:
        in_specs.append(pl.BlockSpec((tm, LANES), lambda i, k: (i, tail_blk)))
        args.append(act_p)
    in_specs.append(pl.BlockSpec((ms, tk), lambda i, k: (0, k)))
    args.append(act_s)
    if has_tail:
        in_specs.append(pl.BlockSpec((ms, LANES), lambda i, k: (0, tail_blk)))
        args.append(act_s)
    in_specs.append(pl.BlockSpec((None, None, tk, d), lambda i, k: (layer, half, k, 0)))
    args.append(w_down)
    if has_tail:
        in_specs.append(pl.BlockSpec((None, None, LANES, d), lambda i, k: (layer, half, tail_blk, 0)))
        args.append(w_down)
    row_spec = pl.BlockSpec((tm, d), lambda i, k: (i, 0))
    srow_spec = pl.BlockSpec((ms, d), lambda i, k: (0, 0))
    vec_spec = pl.BlockSpec((1, d), lambda i, k: (0, 0))
    in_specs += [row_spec, srow_spec, vec_spec]
    args += [h_p, h_s, g_post.reshape(1, d)]
    out_specs = [row_spec]
    out_shape = [jax.ShapeDtypeStruct((mp, d), F32)]
    if has_next:
        in_specs.append(vec_spec)
        args.append(g_next.reshape(1, d))
        out_specs.append(row_spec)
        out_shape.append(jax.ShapeDtypeStruct((mp, d), u_dtype))
    out_specs.append(srow_spec)
    out_shape.append(jax.ShapeDtypeStruct((ms, d), F32))
    if has_next:
        out_specs.append(srow_spec)
        out_shape.append(jax.ShapeDtypeStruct((ms, d), u_dtype))
    kern = functools.partial(_ffn_down_kernel, nk=nk, scale=0.5, has_tail=has_tail, has_next=has_next)
    res = pl.pallas_call(
        kern,
        grid=(mp // tm, nk),
        in_specs=in_specs,
        out_specs=out_specs,
        out_shape=out_shape,
        scratch_shapes=[pltpu.VMEM((tm, d), F32), pltpu.VMEM((ms, d), F32)],
        compiler_params=_params(56, ("arbitrary", "arbitrary")),
        name="ffn_down",
    )(*args)
    if has_next:
        return res[0], res[1], res[2], res[3]
    return res[0], None, res[1], None


def _mixer_out_kernel(xp, xs, *refs, nn, tn, glu):
    refs = list(refs)
    take = lambda cond=True: refs.pop(0) if cond else None
    wa_ref, wb_ref = take(), take(glu)
    hp, hs, gp_ref, gn_ref = take(), take(), take(), take()
    hnp, unp, hns, uns = take(), take(), take(), take()
    accp, accs = take(), take()
    prompt = _RowRefs(xp, None, hp, hnp, unp, accp)
    sample = _RowRefs(xs, None, hs, hns, uns, accs)
    i = pl.program_id(0)
    j = pl.program_id(1)
    col = pl.multiple_of(j * tn, tn)

    def project(r):
        x = r.x[...].astype(BF16)
        if glu:
            y = _dot(x, wa_ref[...]) * jax.nn.sigmoid(_dot(x, wb_ref[...]))
        else:
            y = _dot(x, wa_ref[...])
        r.acc[:, pl.ds(col, tn)] = y

        @pl.when(j == nn - 1)
        def _():
            _residual_epilogue(r.acc[...], r.h, gp_ref, gn_ref, r.hn, r.un, 1.0)

    project(prompt)
    pl.when(i == 0)(lambda: project(sample))


def _mixer_out(x_p, x_s, w, layer, glu, h_p, h_s, g_post, g_next, tm, tn):
    mp, d = h_p.shape
    ms = h_s.shape[0]
    kdim = x_s.shape[1]
    nn = d // tn
    row_spec = pl.BlockSpec((tm, d), lambda i, j: (i, 0))
    srow_spec = pl.BlockSpec((ms, d), lambda i, j: (0, 0))
    vec_spec = pl.BlockSpec((1, d), lambda i, j: (0, 0))
    in_specs = [pl.BlockSpec((tm, kdim), lambda i, j: (i, 0)),
                pl.BlockSpec((ms, kdim), lambda i, j: (0, 0)),
                pl.BlockSpec((None, kdim, tn), lambda i, j: (layer, 0, j))]
    args = [x_p, x_s, w]
    if glu:
        in_specs.append(pl.BlockSpec((None, kdim, tn), lambda i, j: (layer, 0, j + nn)))
        args.append(w)
    in_specs += [row_spec, srow_spec, vec_spec, vec_spec]
    args += [h_p, h_s, g_post.reshape(1, d), g_next.reshape(1, d)]
    return pl.pallas_call(
        functools.partial(_mixer_out_kernel, nn=nn, tn=tn, glu=glu),
        grid=(mp // tm, nn),
        in_specs=in_specs,
        out_specs=[row_spec, row_spec, srow_spec, srow_spec],
        out_shape=[jax.ShapeDtypeStruct((mp, d), F32), jax.ShapeDtypeStruct((mp, d), BF16),
                   jax.ShapeDtypeStruct((ms, d), F32), jax.ShapeDtypeStruct((ms, d), BF16)],
        scratch_shapes=[pltpu.VMEM((tm, d), F32), pltpu.VMEM((ms, d), F32)],
        compiler_params=_params(56, ("arbitrary", "arbitrary")),
        name="mixer_out_glu" if glu else "mixer_out",
    )(*args)


def _log_sigmoid(x):
    return jnp.minimum(x, 0.0) - jnp.log1p(jnp.exp(-jnp.abs(x)))


def _qkv_kernel(x_ref, w_ref, wf_ref, bf_ref, o_ref, lf_ref, *, n_heads):
    j = pl.program_id(1)
    x = x_ref[...]
    o_ref[...] = _dot(x, w_ref[...])

    @pl.when(j == 0)
    def _():
        f = _dot(x, wf_ref[...])[:, :n_heads] + bf_ref[...]
        lf_ref[...] = _log_sigmoid(f)


def _qkv_proj(u, w_in, b_f, layer, tm, tn):
    m, d = u.shape
    n_heads = b_f.shape[-1]
    n_main = 3 * d
    return pl.pallas_call(
        functools.partial(_qkv_kernel, n_heads=n_heads),
        grid=(m // tm, n_main // tn),
        in_specs=[
            pl.BlockSpec((tm, d), lambda i, j: (i, 0)),
            pl.BlockSpec((None, d, tn), lambda i, j: (layer, 0, j)),
            pl.BlockSpec((None, d, LANES), lambda i, j: (layer, 0, n_main // LANES)),
            pl.BlockSpec((None, 1, n_heads), lambda i, j: (layer, 0, 0)),
        ],
        out_specs=[pl.BlockSpec((tm, tn), lambda i, j: (i, j)),
                   pl.BlockSpec((tm, n_heads), lambda i, j: (i, 0))],
        out_shape=[jax.ShapeDtypeStruct((m, n_main), F32), jax.ShapeDtypeStruct((m, n_heads), F32)],
        compiler_params=_params(48, ("parallel", "arbitrary")),
        name="qkv_proj",
    )(u, w_in, w_in, b_f.reshape(b_f.shape[0], 1, n_heads))


def _cumsum_kernel(lf_ref, o_ref, *, t_len, chunk, n_heads):
    r = lax.broadcasted_iota(jnp.int32, (chunk, chunk), 0)
    c = lax.broadcasted_iota(jnp.int32, (chunk, chunk), 1)
    tri = jnp.where(c <= r, 1.0, 0.0).astype(BF16)
    carry = jnp.zeros((1, n_heads), F32)
    for s in range(t_len // chunk):
        x = lf_ref[s * chunk:(s + 1) * chunk, :]
        hi, mid, lo = _split3(x)
        cs = _dot(tri, hi) + _dot(tri, mid) + _dot(tri, lo) + carry
        carry = cs[chunk - 1:chunk, :]
        chi, cmid, clo = _split3(cs)
        o_ref[s * chunk:(s + 1) * chunk, 0:n_heads] = chi
        o_ref[s * chunk:(s + 1) * chunk, n_heads:2 * n_heads] = cmid
        o_ref[s * chunk:(s + 1) * chunk, 2 * n_heads:3 * n_heads] = clo


def _cumsum_chunk(t_len):
    for c in (688, 256, 128, 64, 48, 16):
        if t_len % c == 0:
            return c
    return t_len


def _prompt_cumsum(logf, batch, t_len):
    n_heads = logf.shape[1]
    kern = functools.partial(_cumsum_kernel, t_len=t_len, chunk=_cumsum_chunk(t_len), n_heads=n_heads)
    return pl.pallas_call(
        kern,
        grid=(batch,),
        in_specs=[pl.BlockSpec((t_len, n_heads), lambda b: (b, 0))],
        out_specs=pl.BlockSpec((t_len, 3 * n_heads), lambda b: (b, 0)),
        out_shape=jax.ShapeDtypeStruct((batch * t_len, 3 * n_heads), BF16),
        compiler_params=_params(32, ("parallel",)),
        name="prompt_cumsum",
    )(logf)


def _store_transposed(src_ref, dst_ref, t_len):
    n_full = t_len // LANES
    for c in range(n_full):
        dst_ref[:, c * LANES:(c + 1) * LANES] = src_ref[c * LANES:(c + 1) * LANES, :].T
    rem = t_len - n_full * LANES
    if rem:
        last = src_ref[t_len - LANES:t_len, :].T
        dst_ref[:, n_full * LANES:t_len] = last[:, LANES - rem:]


def _attn_prompt_kernel(q_ref, k_ref, v_ref, cp_ref, *rest, t_len, n_meta, tq, n_heads, head_dim,
                        aliased):
    if aliased:
        rest = rest[2:]
    o_ref, kt_ref, vt_ref, qa_scr, ka_scr, v_scr = rest
    _store_transposed(k_ref, kt_ref, t_len)
    _store_transposed(v_ref, vt_ref, t_len)
    hp = pl.program_id(1)
    half = LANES // 2
    lane = lax.broadcasted_iota(jnp.int32, (1, LANES), 1)
    r3 = lax.broadcasted_iota(jnp.int32, (3 * n_heads, LANES), 0)
    l3 = lax.broadcasted_iota(jnp.int32, (3 * n_heads, LANES), 1)
    cp = cp_ref[...]
    q = (q_ref[...] * (head_dim ** -0.5)).astype(BF16)
    k = k_ref[...].astype(BF16)
    v_scr[...] = v_ref[...].astype(BF16)
    for hh in range(2):
        head = 2 * hp + hh
        base = half * (1 - hh)
        sel_q = jnp.zeros((3 * n_heads, LANES), F32)
        sel_k = jnp.zeros((3 * n_heads, LANES), F32)
        for part in range(3):
            row_hit = r3 == head + part * n_heads
            sel_q = sel_q + jnp.where(row_hit & (l3 == base + part), 1.0, 0.0)
            sel_k = sel_k + jnp.where(row_hit & (l3 == base + 3 + part), -1.0, 0.0)
        ones_q = jnp.where((lane >= base + 3) & (lane < base + 6), 1.0, 0.0)
        ones_k = jnp.where((lane >= base) & (lane < base + 3), 1.0, 0.0)
        ex_q = (_dot(cp, sel_q.astype(BF16)) + ones_q).astype(BF16)
        ex_k = (_dot(cp, sel_k.astype(BF16)) + ones_k).astype(BF16)
        data = (lane >= half * hh) & (lane < half * hh + half)
        qa_scr[hh] = jnp.where(data, q, ex_q)
        ka_scr[hh] = jnp.where(data, k, ex_k)

    tiles = [(0, n_meta)] + [(n_meta + i * tq, tq) for i in range((t_len - n_meta) // tq)]
    for q0, tl in tiles:
        row = lax.broadcasted_iota(jnp.int32, (tl, tl), 0)
        col = lax.broadcasted_iota(jnp.int32, (tl, tl), 1)
        causal = col <= row
        outs = []
        for hh in range(2):
            qt = qa_scr[hh, q0:q0 + tl, :]
            sd = jnp.where(causal, _dot_nt(qt, ka_scr[hh, q0:q0 + tl, :]), -jnp.inf)
            m = jnp.max(sd, axis=-1, keepdims=True)
            if q0 > 0:
                so = _dot_nt(qt, ka_scr[hh, 0:q0, :])
                m = jnp.maximum(m, jnp.max(so, axis=-1, keepdims=True))
            pd = jnp.exp(sd - m)
            den = jnp.sum(pd, axis=-1, keepdims=True)
            o = _dot(pd.astype(BF16), v_scr[q0:q0 + tl, :])
            if q0 > 0:
                po = jnp.exp(so - m)
                den = den + jnp.sum(po, axis=-1, keepdims=True)
                o = o + _dot(po.astype(BF16), v_scr[0:q0, :])
            outs.append(o / den)
        o_ref[q0:q0 + tl, :] = jnp.where(lane < half, outs[0], outs[1]).astype(o_ref.dtype)


def _attn_prompt(qkv, cparts, batch, t_len, n_meta, n_heads, tq, layer, n_layers, kv_t):
    d = qkv.shape[1] // 3
    head_dim = d // n_heads
    assert 2 * head_dim == LANES, "head pairs must fill one lane tile"
    assert (t_len - n_meta) % tq == 0 and t_len >= LANES
    npair = n_heads // 2
    aliased = kv_t is not None
    kern = functools.partial(_attn_prompt_kernel, t_len=t_len, n_meta=n_meta, tq=tq,
                             n_heads=n_heads, head_dim=head_dim, aliased=aliased)
    blk = lambda off: pl.BlockSpec((t_len, LANES), lambda b, p: (b, off + p))
    in_specs = [blk(0), blk(npair), blk(2 * npair),
                pl.BlockSpec((t_len, 3 * n_heads), lambda b, p: (b, 0))]
    args = [qkv, qkv, qkv, cparts]
    aliases = {}
    if aliased:
        in_specs += [pl.BlockSpec(memory_space=pl.ANY), pl.BlockSpec(memory_space=pl.ANY)]
        args += list(kv_t)
        aliases = {4: 1, 5: 2}
    t_spec = pl.BlockSpec((None, None, LANES, t_len), lambda b, p: (layer, b, p, 0))
    t_shape = jax.ShapeDtypeStruct((n_layers, batch, d, t_len), F32)
    y, k_t, v_t = pl.pallas_call(
        kern,
        grid=(batch, npair),
        in_specs=in_specs,
        out_specs=[pl.BlockSpec((t_len, LANES), lambda b, p: (b, p)), t_spec, t_spec],
        out_shape=[jax.ShapeDtypeStruct((batch * t_len, d), BF16), t_shape, t_shape],
        scratch_shapes=[pltpu.VMEM((2, t_len, LANES), BF16), pltpu.VMEM((2, t_len, LANES), BF16),
                        pltpu.VMEM((t_len, LANES), BF16)],
        input_output_aliases=aliases,
        compiler_params=_params(48, ("parallel", "arbitrary")),
        name="attn_prompt",
    )(*args)
    return y, (k_t, v_t)


def _attn_decode_kernel(pt_ref, q_ref, kn_ref, vn_ref, lfn_ref, *rest,
                        n_pg, n_heads, head_dim, n_new, page, n_steps):
    del pt_ref
    k_refs = rest[:n_pg]
    v_refs = rest[n_pg:2 * n_pg]
    lf_refs = rest[2 * n_pg:3 * n_pg]
    o_ref = rest[3 * n_pg]
    (q_scr, s_scr, p_scr, acc_scr, m_scr, l_scr, cnew_scr, alpha_scr,
     tail_scr, b_scr) = rest[3 * n_pg + 1:]
    step = pl.program_id(1)
    rows = n_heads * n_new
    head_unroll = 8 if n_heads % 8 == 0 else 1
    er = lax.broadcasted_iota(jnp.int32, (rows, n_heads), 0)
    ec = lax.broadcasted_iota(jnp.int32, (rows, n_heads), 1)
    expand = jnp.where((er >= ec * n_new) & (er < (ec + 1) * n_new), 1.0, 0.0).astype(BF16)

    def spread(lf):
        return [_dot_nt(expand, part).astype(BF16) for part in _split3(lf)]

    def head_update(h8, p_h, v_h, first):
        contrib = _dot(p_h, v_h)
        if first:
            acc_scr[h8, :] = contrib
        else:
            acc_scr[h8, :] = alpha_scr[h8, 0:head_dim] * acc_scr[h8, :] + contrib

    @pl.when(step == 0)
    def _():
        scale = head_dim ** -0.5
        for h in range(n_heads):
            q_scr[h * n_new:(h + 1) * n_new, :] = q_ref[:, h * head_dim:(h + 1) * head_dim] * scale
        kr = lax.broadcasted_iota(jnp.int32, (n_new, LANES), 0)
        kc = lax.broadcasted_iota(jnp.int32, (n_new, LANES), 1)
        upper = jnp.where(kr <= kc, 1.0, 0.0).astype(BF16)
        parts = spread(lfn_ref[...])
        cum = _dot(parts[0], upper) + _dot(parts[1], upper) + _dot(parts[2], upper)
        rq = lax.broadcasted_iota(jnp.int32, (rows, LANES), 0)
        lq = lax.broadcasted_iota(jnp.int32, (rows, LANES), 1)
        qpos = rq & (n_new - 1)
        c_q = jnp.sum(jnp.where(lq == qpos, cum, 0.0), axis=-1, keepdims=True)
        cnew_scr[...] = jnp.broadcast_to(c_q, (rows, LANES))
        bias = c_q - cum
        valid = lq <= qpos
        s_scr[:, 0:LANES] = jnp.zeros((rows, LANES), F32)
        for h in range(n_heads):
            h8 = pl.ds(h * n_new, n_new)
            k_h = kn_ref[:, h * head_dim:(h + 1) * head_dim].astype(BF16)
            s_scr[h8, 0:n_new] = _dot_nt(q_scr[h8, :].astype(BF16), k_h)
        s = jnp.where(valid, s_scr[:, 0:LANES] + bias, -jnp.inf)
        m = jnp.max(s, axis=-1, keepdims=True)
        p = jnp.exp(s - m)
        m_scr[...] = jnp.broadcast_to(m, (rows, LANES))
        l_scr[...] = p
        tail_scr[...] = jnp.zeros((n_heads, LANES), F32)
        p_scr[:, 0:LANES] = p
        for h in range(n_heads):
            h8 = pl.ds(h * n_new, n_new)
            v_h = vn_ref[:, h * head_dim:(h + 1) * head_dim].astype(BF16)
            head_update(h8, p_scr[h8, 0:n_new].astype(BF16), v_h, True)

    kr = lax.broadcasted_iota(jnp.int32, (page, 2 * page), 0)
    kc = lax.broadcasted_iota(jnp.int32, (page, 2 * page), 1)
    after = jnp.where((kr > kc) | (kc >= page), 1.0, 0.0).astype(BF16)

    tail = tail_scr[...]
    for j in range(n_pg):
        hi, mid, lo = _split3(lf_refs[j][...])
        full = _dot(hi, after) + _dot(mid, after) + _dot(lo, after)
        b_scr[:, j * page:(j + 1) * page] = full[:, 0:page] + tail
        tail = tail + full[:, page:2 * page]
    tail_scr[...] = tail

    def scores(h, carry):
        h8 = pl.ds(pl.multiple_of(h * n_new, n_new), n_new)
        q_h = q_scr[h8, :].astype(BF16)
        k_h = jnp.concatenate([k_refs[j][h] for j in range(n_pg)], axis=1).astype(BF16)
        s_scr[h8, :] = _dot(q_h, k_h) + b_scr[pl.ds(h, 1), :]
        return carry

    lax.fori_loop(0, n_heads, scores, 0, unroll=head_unroll)

    cnew = cnew_scr[...]
    s_all = [s_scr[:, j * page:(j + 1) * page] + cnew for j in range(n_pg)]
    m_chunk = s_all[0]
    for j in range(1, n_pg):
        m_chunk = jnp.maximum(m_chunk, s_all[j])
    m_old = m_scr[...]
    m_new = jnp.maximum(m_old, jnp.max(m_chunk, axis=-1, keepdims=True))
    alpha = jnp.exp(m_old - m_new)
    m_scr[...] = m_new
    alpha_scr[...] = alpha
    l_new = alpha * l_scr[...]
    for j in range(n_pg):
        p = jnp.exp(s_all[j] - m_new)
        l_new = l_new + p
        p_scr[:, j * page:(j + 1) * page] = p
    l_scr[...] = l_new

    def values(h, carry):
        h8 = pl.ds(pl.multiple_of(h * n_new, n_new), n_new)
        v_h = jnp.concatenate([v_refs[j][h] for j in range(n_pg)], axis=1).astype(BF16)
        contrib = _dot_nt(p_scr[h8, :].astype(BF16), v_h)
        acc_scr[h8, :] = alpha_scr[h8, 0:head_dim] * acc_scr[h8, :] + contrib
        return carry

    lax.fori_loop(0, n_heads, values, 0, unroll=head_unroll)

    @pl.when(step == n_steps - 1)
    def _():
        den = jnp.sum(l_scr[...], axis=-1, keepdims=True)
        res = acc_scr[...] / den
        for h in range(n_heads):
            o_ref[:, h * head_dim:(h + 1) * head_dim] = res[h * n_new:(h + 1) * n_new, :]


def _attn_decode(qkv_s, logf_s, cache_k, cache_v, cache_logf, page_table, layer, n_pg):
    n_layers, n_pool, page, n_heads, head_dim = cache_k.shape
    db, n_pages = page_table.shape
    n_new = qkv_s.shape[0] // db
    d = n_heads * head_dim
    rows = n_heads * n_new
    n_steps = n_pages // n_pg
    ck = cache_k.transpose(0, 1, 3, 4, 2)
    cv = cache_v.transpose(0, 1, 3, 4, 2)
    clf = cache_logf.transpose(0, 1, 3, 2)

    def page_of(b, s, pt, j):
        return pt[b * n_pages + (n_pages - 1 - (s * n_pg + j))]

    kv_spec = lambda j: pl.BlockSpec((None, None, n_heads, head_dim, page),
                                     lambda b, s, pt: (layer, page_of(b, s, pt, j), 0, 0, 0))
    lf_spec = lambda j: pl.BlockSpec((None, None, n_heads, page),
                                     lambda b, s, pt: (layer, page_of(b, s, pt, j), 0, 0))
    new_spec = lambda c: pl.BlockSpec((n_new, d), lambda b, s, pt: (b, c))
    in_specs = [new_spec(0), new_spec(1), new_spec(2),
                pl.BlockSpec((n_new, n_heads), lambda b, s, pt: (b, 0))]
    in_specs += [kv_spec(j) for j in range(n_pg)] + [kv_spec(j) for j in range(n_pg)]
    in_specs += [lf_spec(j) for j in range(n_pg)]
    kern = functools.partial(_attn_decode_kernel, n_pg=n_pg, n_heads=n_heads, head_dim=head_dim,
                             n_new=n_new, page=page, n_steps=n_steps)
    stat = pltpu.VMEM((rows, LANES), F32)
    grid_spec = pltpu.PrefetchScalarGridSpec(
        num_scalar_prefetch=1,
        grid=(db, n_steps),
        in_specs=in_specs,
        out_specs=pl.BlockSpec((n_new, d), lambda b, s, pt: (b, 0)),
        scratch_shapes=[pltpu.VMEM((rows, head_dim), F32),
                        pltpu.VMEM((rows, n_pg * page), F32),
                        pltpu.VMEM((rows, n_pg * page), F32),
                        pltpu.VMEM((rows, head_dim), F32),
                        stat, stat, stat, stat,
                        pltpu.VMEM((n_heads, LANES), F32),
                        pltpu.VMEM((n_heads, n_pg * page), F32)],
    )
    return pl.pallas_call(
        kern,
        grid_spec=grid_spec,
        out_shape=jax.ShapeDtypeStruct((db * n_new, d), F32),
        compiler_params=_params(56, ("parallel", "arbitrary")),
        name="attn_decode",
    )(page_table.reshape(-1), qkv_s, qkv_s, qkv_s, logf_s,
      *([ck] * n_pg), *([cv] * n_pg), *([clf] * n_pg))


def _s5_kernel(u_ref, lre_ref, lim_ref, ldt_ref, bre_ref, bim_ref, cre_ref, cim_ref, d_ref,
               x0r_ref, x0i_ref, y_ref, fr_ref, fi_ref,
               w_scr, c_scr, ar_scr, ai_scr, xs_scr, sr_scr, si_scr, *, nb, n_tc, sp):
    tc = pl.program_id(1)
    rows = xs_scr.shape[1]
    steps = rows // nb
    nct = sp // LANES

    def load_cols(row_sel, first, count):
        return jnp.concatenate([xs_scr[first + j, row_sel, :] for j in range(count)], axis=1)

    def store_cols(row_sel, first, val):
        for j in range(val.shape[1] // LANES):
            xs_scr[first + j, row_sel, :] = val[:, j * LANES:(j + 1) * LANES]

    @pl.when(tc == 0)
    def _():
        lre = lre_ref[...]
        lim = lim_ref[...]
        dt = jnp.exp(ldt_ref[...])
        mag = jnp.exp(lre * dt)
        ar = mag * jnp.cos(lim * dt)
        ai = mag * jnp.sin(lim * dt)
        ar_scr[...] = ar
        ai_scr[...] = ai
        xr = ar - 1.0
        den = lre * lre + lim * lim
        cr = (xr * lre + ai * lim) / den
        ci = (ai * lre - xr * lim) / den
        bre = bre_ref[...]
        bim = bim_ref[...]
        w_scr[:, 0:sp] = (bre * cr - bim * ci).astype(BF16)
        w_scr[:, sp:2 * sp] = (bre * ci + bim * cr).astype(BF16)
        c_scr[0:sp, :] = cre_ref[...].astype(BF16)
        c_scr[sp:2 * sp, :] = (-cim_ref[...]).astype(BF16)
        sr_scr[...] = x0r_ref[...]
        si_scr[...] = x0i_ref[...]

    for b in range(nb):
        store_cols(pl.ds(b, steps, stride=nb), 0, _dot(u_ref[b].astype(BF16), w_scr[...]))
    ar8 = jnp.broadcast_to(ar_scr[...], (8, sp))
    ai8 = jnp.broadcast_to(ai_scr[...], (8, sp))
    low = lax.broadcasted_iota(jnp.int32, (8, sp), 0) < 4

    def body(s, carry):
        pr, pi = carry
        r0 = pl.ds(pl.multiple_of(s * 8, 8), 8)
        br = load_cols(r0, 0, nct)
        bi = load_cols(r0, nct, nct)
        er = ar8 * pr - ai8 * pi + br
        ei = ar8 * pi + ai8 * pr + bi
        if nb == 8:
            store_cols(r0, 0, er)
            store_cols(r0, nct, ei)
            return er, ei
        er4 = pltpu.roll(er, 4, 0)
        ei4 = pltpu.roll(ei, 4, 0)
        orr = ar8 * er4 - ai8 * ei4 + br
        oi = ar8 * ei4 + ai8 * er4 + bi
        store_cols(r0, 0, jnp.where(low, er, orr))
        store_cols(r0, nct, jnp.where(low, ei, oi))
        return pltpu.roll(orr, 4, 0), pltpu.roll(oi, 4, 0)

    fr, fi = lax.fori_loop(0, rows // 8, body, (sr_scr[...], si_scr[...]))
    sr_scr[...] = fr
    si_scr[...] = fi
    for b in range(nb):
        x_b = load_cols(pl.ds(b, steps, stride=nb), 0, 2 * nct).astype(BF16)
        y = _dot(x_b, c_scr[...]) + d_ref[...] * u_ref[b]
        y_ref[b] = jax.nn.gelu(y).astype(y_ref.dtype)

    @pl.when(tc == n_tc - 1)
    def _():
        fr_ref[...] = fr
        fi_ref[...] = fi


def _s5_layout(a_re, a_im, b_re, b_im, c_re, c_im, log_dt, gc):
    g, p = a_re.shape
    c = b_re.shape[-1]
    nch = g // gc
    eye = jnp.eye(gc, dtype=F32)

    def bd_in(b):
        b4 = b.reshape(nch, gc, p, c).transpose(0, 1, 3, 2)
        return (b4[:, :, :, None, :] * eye[None, :, None, :, None]).reshape(nch, gc * c, gc * p)

    def bd_out(cm):
        c4 = cm.reshape(nch, gc, c, p).transpose(0, 1, 3, 2)
        return (c4[:, :, :, None, :] * eye[None, :, None, :, None]).reshape(nch, gc * p, gc * c)

    row = lambda x: x.reshape(nch, 1, gc * p)
    return (row(a_re), row(a_im), row(jnp.repeat(log_dt, p)),
            bd_in(b_re), bd_in(b_im), bd_out(c_re), bd_out(c_im))


def _s5_scan(u, layout, d_skip, x0_re, x0_im, nb, steps):
    lre, lim, ldt, bre, bim, cre, cim = layout
    rows_total, d = u.shape
    nch, uc, sp = bre.shape
    t_len = rows_total // nb
    n_tc = t_len // steps
    assert t_len % steps == 0 and (steps * nb) % 8 == 0 and nb in (4, 8)
    kern = functools.partial(_s5_kernel, nb=nb, n_tc=n_tc, sp=sp)
    par = lambda r, c: pl.BlockSpec((None, r, c), lambda g, t: (g, 0, 0))
    st_spec = pl.BlockSpec((8, sp), lambda g, t: (0, g))
    seq_spec = pl.BlockSpec((nb, steps, uc), lambda g, t: (0, t, g))
    y, f_re, f_im = pl.pallas_call(
        kern,
        grid=(nch, n_tc),
        in_specs=[seq_spec, par(1, sp), par(1, sp), par(1, sp),
                  par(uc, sp), par(uc, sp), par(sp, uc), par(sp, uc),
                  pl.BlockSpec((1, uc), lambda g, t: (0, g)),
                  st_spec, st_spec],
        out_specs=[seq_spec, st_spec, st_spec],
        out_shape=[jax.ShapeDtypeStruct((nb, t_len, d), BF16),
                   jax.ShapeDtypeStruct(x0_re.shape, F32), jax.ShapeDtypeStruct(x0_re.shape, F32)],
        scratch_shapes=[pltpu.VMEM((uc, 2 * sp), BF16), pltpu.VMEM((2 * sp, uc), BF16),
                        pltpu.VMEM((1, sp), F32), pltpu.VMEM((1, sp), F32),
                        pltpu.VMEM((2 * sp // LANES, steps * nb, LANES), F32),
                        pltpu.VMEM((8, sp), F32), pltpu.VMEM((8, sp), F32)],
        compiler_params=_params(56, ("parallel", "arbitrary")),
        name="s5_scan",
    )(u.reshape(nb, t_len, d), lre, lim, ldt, bre, bim, cre, cim, d_skip.reshape(1, d), x0_re, x0_im)
    return y.reshape(rows_total, d), f_re, f_im


def _row_tile(m, cap):
    best = None
    for t in range(16, cap + 1, 16):
        if m % t == 0:
            best = t
    return best if best is not None else m


def _col_tile(n, cap):
    best = LANES
    for t in range(LANES, cap + 1, LANES):
        if n % t == 0:
            best = t
    return best


def kernel(x_prompt, x_sample, cache_k, cache_v, cache_logf, state_s5_re, state_s5_im, page_table,
           meta_tokens, norm_g, ffn_w_up, ffn_w_down, attn_w_in, attn_b_f, attn_w_out,
           s5_A_re, s5_A_im, s5_B_re, s5_B_im, s5_C_re, s5_C_im, s5_log_dt, s5_D, s5_w_glu):
    bsz, seq, d = x_prompt.shape
    db, ds, _ = x_sample.shape
    n_meta = meta_tokens.shape[0]
    t_len = n_meta + seq
    mp, ms = bsz * t_len, db * ds
    depth = norm_g.shape[0]
    n_heads = attn_b_f.shape[-1]
    n_groups, state_dim = s5_A_re.shape[1:]

    tm = _row_tile(t_len, 704)
    tn_qkv = _col_tile(3 * d, 1024)
    tn_out = _col_tile(d, 512)
    tq = 256 if seq % 256 == 0 else LANES
    n_pages = page_table.shape[1]
    pages_per_step = max(p for p in (8, 4, 2, 1) if n_pages % p == 0)
    n_attn = (depth + 1) // 2

    w_down = ffn_w_down.astype(BF16)
    w_in = attn_w_in.astype(BF16)
    w_out = attn_w_out.astype(BF16)
    w_glu = s5_w_glu.astype(BF16)

    meta = jnp.broadcast_to(meta_tokens[None].astype(x_prompt.dtype), (bsz, n_meta, d))
    hp = jnp.concatenate([meta, x_prompt], axis=1).reshape(mp, d)
    hs = x_sample.reshape(ms, d)

    lf_pr, sre_pr, sim_pr = [], [], []
    k_sa, v_sa, lf_sa, sre_sa, sim_sa = [], [], [], [], []
    kv_t = None
    up = _rmsnorm(hp, norm_g[0, 0], tm)
    us = _rmsnorm(hs, norm_g[0, 0], ms)
    for i in range(depth):
        g = norm_g[i]
        attn_layer = i % 2 == 0
        act_p, act_s = _ffn_up(up, us, ffn_w_up, i, 0, tm)
        hp, up, hs, us = _ffn_down(act_p, act_s, w_down, i, 0, hp, hs, g[1], g[2],
                                   BF16 if attn_layer else F32, tm)
        if attn_layer:
            li = i // 2
            qkv_p, logf_p = _qkv_proj(up, w_in, attn_b_f, li, tm, tn_qkv)
            qkv_s, logf_s = _qkv_proj(us, w_in, attn_b_f, li, ms, tn_qkv)
            cparts = _prompt_cumsum(logf_p, bsz, t_len)
            yp, kv_t = _attn_prompt(qkv_p, cparts, bsz, t_len, n_meta, n_heads, tq, li, n_attn, kv_t)
            ys = _attn_decode(qkv_s, logf_s, cache_k, cache_v, cache_logf, page_table, li,
                              pages_per_step)
            hp, up, hs, us = _mixer_out(yp, ys, w_out, li, False, hp, hs, g[3], g[4], tm, tn_out)
            lf_pr.append(logf_p)
            k_sa.append(qkv_s[:, d:2 * d]); v_sa.append(qkv_s[:, 2 * d:]); lf_sa.append(logf_s)
        else:
            si = i // 2
            layout = _s5_layout(s5_A_re[si], s5_A_im[si], s5_B_re[si], s5_B_im[si],
                                s5_C_re[si], s5_C_im[si], s5_log_dt[si], 16)
            zeros = jnp.zeros((8, n_groups * state_dim), F32)
            yp, re_p, im_p = _s5_scan(up, layout, s5_D[si], zeros, zeros, bsz, tm)
            ys, re_s, im_s = _s5_scan(us, layout, s5_D[si],
                                      state_s5_re[si].reshape(db, -1), state_s5_im[si].reshape(db, -1),
                                      db, ds)
            hp, up, hs, us = _mixer_out(yp, ys, w_glu, si, True, hp, hs, g[3], g[4], tm, tn_out)
            sre_pr.append(re_p[:bsz]); sim_pr.append(im_p[:bsz])
            sre_sa.append(re_s[:db]); sim_sa.append(im_s[:db])
        act_p, act_s = _ffn_up(up, us, ffn_w_up, i, 1, tm)
        g_next = norm_g[i + 1, 0] if i + 1 < depth else None
        hp, up, hs, us = _ffn_down(act_p, act_s, w_down, i, 1, hp, hs, g[5], g_next, BF16, tm)

    hd = d // n_heads
    stack = lambda xs, shape: jnp.stack(xs).reshape((len(xs),) + shape)
    kv_out = lambda x: x.reshape(n_attn, bsz, n_heads, hd, t_len).transpose(0, 1, 4, 2, 3)
    return (hp.reshape(bsz, t_len, d)[:, n_meta:], hs.reshape(db, ds, d),
            kv_out(kv_t[0]), kv_out(kv_t[1]),
            stack(lf_pr, (bsz, t_len, n_heads)),
            stack(sre_pr, (bsz, n_groups, state_dim)), stack(sim_pr, (bsz, n_groups, state_dim)),
            stack(k_sa, (db, ds, n_heads, hd)), stack(v_sa, (db, ds, n_heads, hd)),
            stack(lf_sa, (db, ds, n_heads)),
            stack(sre_sa, (db, n_groups, state_dim)), stack(sim_sa, (db, n_groups, state_dim)))
```

```python
import functools

import jax
import jax.numpy as jnp
from jax import lax
from jax.experimental import pallas as pl
from jax.experimental.pallas import tpu as pltpu

F32 = jnp.float32
BF16 = jnp.bfloat16
RMS_EPS = 1e-6
LOG2E = 1.4426950408889634
LANES = 128
MIB = 1024 * 1024
NT_DIMS = (((1,), (1,)), ((), ()))


def _params(vmem_mib, semantics=None):
    return pltpu.CompilerParams(dimension_semantics=semantics, vmem_limit_bytes=vmem_mib * MIB)


def _rms(x, g):
    ms = jnp.mean(x * x, axis=-1, keepdims=True)
    return (x * lax.rsqrt(ms + RMS_EPS)) * g


def _split3(x):
    hi = x.astype(BF16)
    r1 = x - hi.astype(F32)
    mid = r1.astype(BF16)
    lo = (r1 - mid.astype(F32)).astype(BF16)
    return hi, mid, lo


def _dot(a, b):
    return jnp.dot(a, b, preferred_element_type=F32)


def _dot_nt(a, b):
    return lax.dot_general(a, b, NT_DIMS, preferred_element_type=F32)


def _rmsnorm_kernel(h_ref, g_ref, o_ref):
    o_ref[...] = _rms(h_ref[...], g_ref[...]).astype(o_ref.dtype)


def _rmsnorm(h, g, tm):
    m, d = h.shape
    return pl.pallas_call(
        _rmsnorm_kernel,
        grid=(m // tm,),
        in_specs=[pl.BlockSpec((tm, d), lambda i: (i, 0)), pl.BlockSpec((1, d), lambda i: (0, 0))],
        out_specs=pl.BlockSpec((tm, d), lambda i: (i, 0)),
        out_shape=jax.ShapeDtypeStruct((m, d), BF16),
        compiler_params=_params(40),
        name="rmsnorm",
    )(h, g.reshape(1, d))


def _ffn_up_kernel(up_ref, us_ref, wa_ref, wb_ref, wd_ref, op_ref, os_ref, wdb_ref, w_scr,
                   *, n_chunks, ch):
    w_scr[:, :LANES] = wa_ref[...].astype(BF16)
    w_scr[:, LANES:] = wb_ref[...].astype(BF16)
    wdb_ref[...] = wd_ref[...].astype(BF16)

    def swiglu(x):
        z = _dot(x, w_scr[...])
        a = z[:, :LANES]
        b = z[:, LANES:]
        return (a * jax.nn.sigmoid(a) * b).astype(BF16)

    def body(c, carry):
        rows = pl.ds(pl.multiple_of(c * ch, ch), ch)
        op_ref[rows, :] = swiglu(up_ref[rows, :])
        return carry

    lax.fori_loop(0, n_chunks, body, 0, unroll=True)
    os_ref[...] = swiglu(us_ref[...])


def _ffn_up(up, us, w_up, w_down, layer, half, ch):
    mp, d = up.shape
    ms = us.shape[0]
    f = w_up.shape[-1] // 2
    nt = f // LANES
    kern = functools.partial(_ffn_up_kernel, n_chunks=mp // ch, ch=ch)
    return pl.pallas_call(
        kern,
        grid=(nt,),
        in_specs=[
            pl.BlockSpec(memory_space=pltpu.VMEM),
            pl.BlockSpec(memory_space=pltpu.VMEM),
            pl.BlockSpec((None, None, d, LANES), lambda j: (layer, half, 0, j)),
            pl.BlockSpec((None, None, d, LANES), lambda j: (layer, half, 0, j + nt)),
            pl.BlockSpec((None, None, LANES, d), lambda j: (layer, half, j, 0)),
        ],
        out_specs=[pl.BlockSpec((mp, LANES), lambda j: (0, j)),
                   pl.BlockSpec((ms, LANES), lambda j: (0, j)),
                   pl.BlockSpec((LANES, d), lambda j: (j, 0))],
        out_shape=[jax.ShapeDtypeStruct((mp, f), BF16), jax.ShapeDtypeStruct((ms, f), BF16),
                   jax.ShapeDtypeStruct((f, d), BF16)],
        scratch_shapes=[pltpu.VMEM((d, 2 * LANES), BF16)],
        compiler_params=_params(58),
        name="ffn_up",
    )(up, us, w_up, w_up, w_down)


def _residual_epilogue(y, h_ref, gp_ref, gn_ref, hn_ref, un_ref, scale):
    hn = h_ref[...] + scale * _rms(y, gp_ref[...])
    hn_ref[...] = hn
    if un_ref is not None:
        un_ref[...] = _rms(hn, gn_ref[...]).astype(un_ref.dtype)


def _down_tiling(f):
    nt = f // LANES
    for dmul in (8, 6, 4, 2, 7, 5, 3):
        if nt % dmul == 0 and nt // dmul >= 1:
            return dmul * LANES, nt // dmul, False
    for dmul in (8, 6, 4, 2, 7, 5, 3):
        if (nt - 1) % dmul == 0:
            return dmul * LANES, (nt - 1) // dmul, True
    return LANES, nt, False


class _RowRefs:
    def __init__(self, x, xt, h, hn, un, acc):
        self.x, self.xt, self.h, self.hn, self.un, self.acc = x, xt, h, hn, un, acc


def _unpack_rows(refs, has_tail, has_next):
    refs = list(refs)
    take = lambda cond=True: refs.pop(0) if cond else None
    xp, xtp, xs, xts = take(), take(has_tail), take(), take(has_tail)
    return refs, xp, xtp, xs, xts


def _ffn_down_kernel(*refs, nk, scale, has_tail, has_next):
    refs, xp, xtp, xs, xts = _unpack_rows(refs, has_tail, has_next)
    take = lambda cond=True: refs.pop(0) if cond else None
    wm_ref, wt_ref = take(), take(has_tail)
    hp, hs, gp_ref, gn_ref = take(), take(), take(), take(has_next)
    hnp, unp, hns, uns = take(), take(has_next), take(), take(has_next)
    accp, accs = take(), take()
    prompt = _RowRefs(xp, xtp, hp, hnp, unp, accp)
    sample = _RowRefs(xs, xts, hs, hns, uns, accs)
    i = pl.program_id(0)
    k = pl.program_id(1)

    def accumulate(r):
        @pl.when(k == 0)
        def _():
            if has_tail:
                r.acc[...] = _dot(r.xt[...], wt_ref[...])
            else:
                r.acc[...] = jnp.zeros(r.acc.shape, F32)

        r.acc[...] += _dot(r.x[...], wm_ref[...])

        @pl.when(k == nk - 1)
        def _():
            _residual_epilogue(r.acc[...], r.h, gp_ref, gn_ref, r.hn, r.un, scale)

    accumulate(prompt)
    pl.when(i == 0)(lambda: accumulate(sample))


def _ffn_down(act_p, act_s, w_down, h_p, h_s, g_post, g_next, u_dtype, tm):
    mp, f = act_p.shape
    ms = act_s.shape[0]
    d = h_p.shape[1]
    tk, nk, has_tail = _down_tiling(f)
    has_next = g_next is not None
    tail_blk = (f - LANES) // LANES
    in_specs = [pl.BlockSpec((tm, tk), lambda i, k: (i, k))]
    args = [act_p]
    if has_tail:
        in_specs.append(pl.BlockSpec((tm, LANES), lambda i, k: (i, tail_blk)))
        args.append(act_p)
    in_specs.append(pl.BlockSpec((ms, tk), lambda i, k: (0, k)))
    args.append(act_s)
    if has_tail:
        in_specs.append(pl.BlockSpec((ms, LANES), lambda i, k: (0, tail_blk)))
        args.append(act_s)
    in_specs.append(pl.BlockSpec((tk, d), lambda i, k: (k, 0)))
    args.append(w_down)
    if has_tail:
        in_specs.append(pl.BlockSpec((LANES, d), lambda i, k: (tail_blk, 0)))
        args.append(w_down)
    row_spec = pl.BlockSpec((tm, d), lambda i, k: (i, 0))
    srow_spec = pl.BlockSpec((ms, d), lambda i, k: (0, 0))
    vec_spec = pl.BlockSpec((1, d), lambda i, k: (0, 0))
    in_specs += [row_spec, srow_spec, vec_spec]
    args += [h_p, h_s, g_post.reshape(1, d)]
    out_specs = [row_spec]
    out_shape = [jax.ShapeDtypeStruct((mp, d), F32)]
    if has_next:
        in_specs.append(vec_spec)
        args.append(g_next.reshape(1, d))
        out_specs.append(row_spec)
        out_shape.append(jax.ShapeDtypeStruct((mp, d), u_dtype))
    out_specs.append(srow_spec)
    out_shape.append(jax.ShapeDtypeStruct((ms, d), F32))
    if has_next:
        out_specs.append(srow_spec)
        out_shape.append(jax.ShapeDtypeStruct((ms, d), u_dtype))
    kern = functools.partial(_ffn_down_kernel, nk=nk, scale=0.5, has_tail=has_tail, has_next=has_next)
    res = pl.pallas_call(
        kern,
        grid=(mp // tm, nk),
        in_specs=in_specs,
        out_specs=out_specs,
        out_shape=out_shape,
        scratch_shapes=[pltpu.VMEM((tm, d), F32), pltpu.VMEM((ms, d), F32)],
        compiler_params=_params(56, ("arbitrary", "arbitrary")),
        name="ffn_down",
    )(*args)
    if has_next:
        return res[0], res[1], res[2], res[3]
    return res[0], None, res[1], None


def _mixer_out_kernel(xp, xs, *refs, nn, tn, glu):
    refs = list(refs)
    take = lambda cond=True: refs.pop(0) if cond else None
    wa_ref, wb_ref = take(), take(glu)
    hp, hs, gp_ref, gn_ref = take(), take(), take(), take()
    hnp, unp, hns, uns = take(), take(), take(), take()
    accp, accs = take(), take()
    prompt = _RowRefs(xp, None, hp, hnp, unp, accp)
    sample = _RowRefs(xs, None, hs, hns, uns, accs)
    i = pl.program_id(0)
    j = pl.program_id(1)
    col = pl.multiple_of(j * tn, tn)

    def project(r):
        x = r.x[...].astype(BF16)
        if glu:
            y = _dot(x, wa_ref[...]) * jax.nn.sigmoid(_dot(x, wb_ref[...]))
        else:
            y = _dot(x, wa_ref[...])
        r.acc[:, pl.ds(col, tn)] = y

        @pl.when(j == nn - 1)
        def _():
            _residual_epilogue(r.acc[...], r.h, gp_ref, gn_ref, r.hn, r.un, 1.0)

    project(prompt)
    pl.when(i == 0)(lambda: project(sample))


def _mixer_out(x_p, x_s, w, layer, glu, h_p, h_s, g_post, g_next, tm, tn):
    mp, d = h_p.shape
    ms = h_s.shape[0]
    kdim = x_s.shape[1]
    nn = d // tn
    row_spec = pl.BlockSpec((tm, d), lambda i, j: (i, 0))
    srow_spec = pl.BlockSpec((ms, d), lambda i, j: (0, 0))
    vec_spec = pl.BlockSpec((1, d), lambda i, j: (0, 0))
    in_specs = [pl.BlockSpec((tm, kdim), lambda i, j: (i, 0)),
                pl.BlockSpec((ms, kdim), lambda i, j: (0, 0)),
                pl.BlockSpec((None, kdim, tn), lambda i, j: (layer, 0, j))]
    args = [x_p, x_s, w]
    if glu:
        in_specs.append(pl.BlockSpec((None, kdim, tn), lambda i, j: (layer, 0, j + nn)))
        args.append(w)
    in_specs += [row_spec, srow_spec, vec_spec, vec_spec]
    args += [h_p, h_s, g_post.reshape(1, d), g_next.reshape(1, d)]
    return pl.pallas_call(
        functools.partial(_mixer_out_kernel, nn=nn, tn=tn, glu=glu),
        grid=(mp // tm, nn),
        in_specs=in_specs,
        out_specs=[row_spec, row_spec, srow_spec, srow_spec],
        out_shape=[jax.ShapeDtypeStruct((mp, d), F32), jax.ShapeDtypeStruct((mp, d), BF16),
                   jax.ShapeDtypeStruct((ms, d), F32), jax.ShapeDtypeStruct((ms, d), BF16)],
        scratch_shapes=[pltpu.VMEM((tm, d), F32), pltpu.VMEM((ms, d), F32)],
        compiler_params=_params(56, ("arbitrary", "arbitrary")),
        name="mixer_out_glu" if glu else "mixer_out",
    )(*args)


def _log_sigmoid(x):
    return jnp.minimum(x, 0.0) - jnp.log1p(jnp.exp(-jnp.abs(x)))


def _qkv_kernel(x_ref, w_ref, wf_ref, bf_ref, o_ref, lf_ref, *, n_heads):
    j = pl.program_id(1)
    x = x_ref[...]
    o_ref[...] = _dot(x, w_ref[...])

    @pl.when(j == 0)
    def _():
        f = _dot(x, wf_ref[...])[:, :n_heads] + bf_ref[...]
        lf_ref[...] = _log_sigmoid(f)


def _qkv_proj(u, w_in, b_f, layer, tm, tn):
    m, d = u.shape
    n_heads = b_f.shape[-1]
    n_main = 3 * d
    return pl.pallas_call(
        functools.partial(_qkv_kernel, n_heads=n_heads),
        grid=(m // tm, n_main // tn),
        in_specs=[
            pl.BlockSpec((tm, d), lambda i, j: (i, 0)),
            pl.BlockSpec((None, d, tn), lambda i, j: (layer, 0, j)),
            pl.BlockSpec((None, d, LANES), lambda i, j: (layer, 0, n_main // LANES)),
            pl.BlockSpec((None, 1, n_heads), lambda i, j: (layer, 0, 0)),
        ],
        out_specs=[pl.BlockSpec((tm, tn), lambda i, j: (i, j)),
                   pl.BlockSpec((tm, n_heads), lambda i, j: (i, 0))],
        out_shape=[jax.ShapeDtypeStruct((m, n_main), F32), jax.ShapeDtypeStruct((m, n_heads), F32)],
        compiler_params=_params(48, ("parallel", "arbitrary")),
        name="qkv_proj",
    )(u, w_in, w_in, b_f.reshape(b_f.shape[0], 1, n_heads))


def _cumsum_kernel(lf_ref, o_ref, *, t_len, chunk, n_heads):
    r = lax.broadcasted_iota(jnp.int32, (chunk, chunk), 0)
    c = lax.broadcasted_iota(jnp.int32, (chunk, chunk), 1)
    tri = jnp.where(c <= r, 1.0, 0.0).astype(BF16)
    carry = jnp.zeros((1, n_heads), F32)
    for s in range(t_len // chunk):
        x = lf_ref[s * chunk:(s + 1) * chunk, :]
        hi, mid, lo = _split3(x)
        cs = _dot(tri, hi) + _dot(tri, mid) + _dot(tri, lo) + carry
        carry = cs[chunk - 1:chunk, :]
        chi, cmid, clo = _split3(cs * LOG2E)
        o_ref[s * chunk:(s + 1) * chunk, 0:n_heads] = chi
        o_ref[s * chunk:(s + 1) * chunk, n_heads:2 * n_heads] = cmid
        o_ref[s * chunk:(s + 1) * chunk, 2 * n_heads:3 * n_heads] = clo


def _cumsum_chunk(t_len):
    for c in (688, 256, 128, 64, 48, 16):
        if t_len % c == 0:
            return c
    return t_len


def _prompt_cumsum(logf, batch, t_len):
    n_heads = logf.shape[1]
    kern = functools.partial(_cumsum_kernel, t_len=t_len, chunk=_cumsum_chunk(t_len), n_heads=n_heads)
    return pl.pallas_call(
        kern,
        grid=(batch,),
        in_specs=[pl.BlockSpec((t_len, n_heads), lambda b: (b, 0))],
        out_specs=pl.BlockSpec((t_len, 3 * n_heads), lambda b: (b, 0)),
        out_shape=jax.ShapeDtypeStruct((batch * t_len, 3 * n_heads), BF16),
        compiler_params=_params(32, ("parallel",)),
        name="prompt_cumsum",
    )(logf)


def _store_transposed(src_ref, dst_ref, t_len):
    n_full = t_len // LANES
    for c in range(n_full):
        dst_ref[:, c * LANES:(c + 1) * LANES] = src_ref[c * LANES:(c + 1) * LANES, :].T
    rem = t_len - n_full * LANES
    if rem:
        last = src_ref[t_len - LANES:t_len, :].T
        dst_ref[:, n_full * LANES:t_len] = last[:, LANES - rem:]


def _attn_prompt_kernel(q_ref, k_ref, v_ref, cp_ref, *rest, t_len, n_meta, tq, n_heads, head_dim,
                        aliased):
    if aliased:
        rest = rest[2:]
    o_ref, kt_ref, vt_ref, qa_scr, ka_scr, v_scr = rest
    _store_transposed(k_ref, kt_ref, t_len)
    _store_transposed(v_ref, vt_ref, t_len)
    hp = pl.program_id(1)
    half = LANES // 2
    lane = lax.broadcasted_iota(jnp.int32, (1, LANES), 1)
    r3 = lax.broadcasted_iota(jnp.int32, (3 * n_heads, LANES), 0)
    l3 = lax.broadcasted_iota(jnp.int32, (3 * n_heads, LANES), 1)
    cp = cp_ref[...]
    q = (q_ref[...] * (head_dim ** -0.5 * LOG2E)).astype(BF16)
    k = k_ref[...].astype(BF16)
    v_scr[...] = v_ref[...].astype(BF16)
    for hh in range(2):
        head = 2 * hp + hh
        base = half * (1 - hh)
        sel_q = jnp.zeros((3 * n_heads, LANES), F32)
        sel_k = jnp.zeros((3 * n_heads, LANES), F32)
        for part in range(3):
            row_hit = r3 == head + part * n_heads
            sel_q = sel_q + jnp.where(row_hit & (l3 == base + part), 1.0, 0.0)
            sel_k = sel_k + jnp.where(row_hit & (l3 == base + 3 + part), -1.0, 0.0)
        ones_q = jnp.where((lane >= base + 3) & (lane < base + 6), 1.0, 0.0)
        ones_k = jnp.where((lane >= base) & (lane < base + 3), 1.0, 0.0)
        ex_q = (_dot(cp, sel_q.astype(BF16)) + ones_q).astype(BF16)
        ex_k = (_dot(cp, sel_k.astype(BF16)) + ones_k).astype(BF16)
        data = (lane >= half * hh) & (lane < half * hh + half)
        qa_scr[hh] = jnp.where(data, q, ex_q)
        ka_scr[hh] = jnp.where(data, k, ex_k)

    tiles = [(0, n_meta)] + [(n_meta + i * tq, tq) for i in range((t_len - n_meta) // tq)]
    units = [(q0, tl, hh) for q0, tl in tiles for hh in range(2)]

    def scores(q0, tl, hh):
        qt = qa_scr[hh, q0:q0 + tl, :]
        row = lax.broadcasted_iota(jnp.int32, (tl, tl), 0)
        col = lax.broadcasted_iota(jnp.int32, (tl, tl), 1)
        sd = jnp.where(col <= row, _dot_nt(qt, ka_scr[hh, q0:q0 + tl, :]), -jnp.inf)
        so = _dot_nt(qt, ka_scr[hh, 0:q0, :]) if q0 > 0 else None
        return sd, so

    def softmax(sd, so):
        m = jnp.max(sd, axis=-1, keepdims=True)
        if so is not None:
            m = jnp.maximum(m, jnp.max(so, axis=-1, keepdims=True))
        pd = jnp.exp2(sd - m)
        den = jnp.sum(pd, axis=-1, keepdims=True)
        po = None
        if so is not None:
            po = jnp.exp2(so - m)
            den = den + jnp.sum(po, axis=-1, keepdims=True)
            po = po.astype(BF16)
        return pd.astype(BF16), po, den

    def values(q0, tl, pd, po, den):
        o = _dot(pd, v_scr[q0:q0 + tl, :])
        if po is not None:
            o = o + _dot(po, v_scr[0:q0, :])
        return o / den

    staged_s, staged_p, done = {}, {}, {}
    for k in range(len(units) + 2):
        if k < len(units):
            staged_s[k] = scores(*units[k])
        if 0 <= k - 1 < len(units):
            staged_p[k - 1] = softmax(*staged_s.pop(k - 1))
        if 0 <= k - 2 < len(units):
            q0, tl, hh = units[k - 2]
            done[hh] = values(q0, tl, *staged_p.pop(k - 2))
            if hh == 1:
                o_ref[q0:q0 + tl, :] = jnp.where(lane < half, done[0], done[1]).astype(o_ref.dtype)


def _attn_prompt(qkv, cparts, batch, t_len, n_meta, n_heads, tq, layer, n_layers, kv_t):
    d = qkv.shape[1] // 3
    head_dim = d // n_heads
    assert 2 * head_dim == LANES, "head pairs must fill one lane tile"
    assert (t_len - n_meta) % tq == 0 and t_len >= LANES
    npair = n_heads // 2
    aliased = kv_t is not None
    kern = functools.partial(_attn_prompt_kernel, t_len=t_len, n_meta=n_meta, tq=tq,
                             n_heads=n_heads, head_dim=head_dim, aliased=aliased)
    blk = lambda off: pl.BlockSpec((t_len, LANES), lambda b, p: (b, off + p))
    in_specs = [blk(0), blk(npair), blk(2 * npair),
                pl.BlockSpec((t_len, 3 * n_heads), lambda b, p: (b, 0))]
    args = [qkv, qkv, qkv, cparts]
    aliases = {}
    if aliased:
        in_specs += [pl.BlockSpec(memory_space=pl.ANY), pl.BlockSpec(memory_space=pl.ANY)]
        args += list(kv_t)
        aliases = {4: 1, 5: 2}
    t_spec = pl.BlockSpec((None, None, LANES, t_len), lambda b, p: (layer, b, p, 0))
    t_shape = jax.ShapeDtypeStruct((n_layers, batch, d, t_len), F32)
    y, k_t, v_t = pl.pallas_call(
        kern,
        grid=(batch, npair),
        in_specs=in_specs,
        out_specs=[pl.BlockSpec((t_len, LANES), lambda b, p: (b, p)), t_spec, t_spec],
        out_shape=[jax.ShapeDtypeStruct((batch * t_len, d), BF16), t_shape, t_shape],
        scratch_shapes=[pltpu.VMEM((2, t_len, LANES), BF16), pltpu.VMEM((2, t_len, LANES), BF16),
                        pltpu.VMEM((t_len, LANES), BF16)],
        input_output_aliases=aliases,
        compiler_params=_params(48, ("parallel", "arbitrary")),
        name="attn_prompt",
    )(*args)
    return y, (k_t, v_t)


def _attn_decode_kernel(pt_ref, q_ref, kn_ref, vn_ref, lfn_ref, *rest,
                        n_pg, n_heads, head_dim, n_new, page, n_steps):
    del pt_ref
    k_refs = rest[:n_pg]
    v_refs = rest[n_pg:2 * n_pg]
    lf_refs = rest[2 * n_pg:3 * n_pg]
    o_ref = rest[3 * n_pg]
    (q_scr, s_scr, p_scr, acc_scr, m_scr, l_scr, cnew_scr, alpha_scr,
     tail_scr, b_scr) = rest[3 * n_pg + 1:]
    step = pl.program_id(1)
    rows = n_heads * n_new
    head_unroll = 8 if n_heads % 8 == 0 else 1
    er = lax.broadcasted_iota(jnp.int32, (rows, n_heads), 0)
    ec = lax.broadcasted_iota(jnp.int32, (rows, n_heads), 1)
    expand = jnp.where((er >= ec * n_new) & (er < (ec + 1) * n_new), 1.0, 0.0).astype(BF16)

    def spread(lf):
        return [_dot_nt(expand, part).astype(BF16) for part in _split3(lf)]

    def head_update(h8, p_h, v_h, first):
        contrib = _dot(p_h, v_h)
        if first:
            acc_scr[h8, :] = contrib
        else:
            acc_scr[h8, :] = alpha_scr[h8, 0:head_dim] * acc_scr[h8, :] + contrib

    @pl.when(step == 0)
    def _():
        scale = head_dim ** -0.5
        for h in range(n_heads):
            q_scr[h * n_new:(h + 1) * n_new, :] = q_ref[:, h * head_dim:(h + 1) * head_dim] * scale
        kr = lax.broadcasted_iota(jnp.int32, (n_new, LANES), 0)
        kc = lax.broadcasted_iota(jnp.int32, (n_new, LANES), 1)
        upper = jnp.where(kr <= kc, 1.0, 0.0).astype(BF16)
        parts = spread(lfn_ref[...])
        cum = _dot(parts[0], upper) + _dot(parts[1], upper) + _dot(parts[2], upper)
        rq = lax.broadcasted_iota(jnp.int32, (rows, LANES), 0)
        lq = lax.broadcasted_iota(jnp.int32, (rows, LANES), 1)
        qpos = rq & (n_new - 1)
        c_q = jnp.sum(jnp.where(lq == qpos, cum, 0.0), axis=-1, keepdims=True)
        cnew_scr[...] = jnp.broadcast_to(c_q, (rows, LANES))
        bias = c_q - cum
        valid = lq <= qpos
        s_scr[:, 0:LANES] = jnp.zeros((rows, LANES), F32)
        for h in range(n_heads):
            h8 = pl.ds(h * n_new, n_new)
            k_h = kn_ref[:, h * head_dim:(h + 1) * head_dim].astype(BF16)
            s_scr[h8, 0:n_new] = _dot_nt(q_scr[h8, :].astype(BF16), k_h)
        s = jnp.where(valid, s_scr[:, 0:LANES] + bias, -jnp.inf)
        m = jnp.max(s, axis=-1, keepdims=True)
        p = jnp.exp(s - m)
        m_scr[...] = jnp.broadcast_to(m, (rows, LANES))
        l_scr[...] = p
        tail_scr[...] = jnp.zeros((n_heads, LANES), F32)
        p_scr[:, 0:LANES] = p
        for h in range(n_heads):
            h8 = pl.ds(h * n_new, n_new)
            v_h = vn_ref[:, h * head_dim:(h + 1) * head_dim].astype(BF16)
            head_update(h8, p_scr[h8, 0:n_new].astype(BF16), v_h, True)

    kr = lax.broadcasted_iota(jnp.int32, (page, 2 * page), 0)
    kc = lax.broadcasted_iota(jnp.int32, (page, 2 * page), 1)
    after = jnp.where((kr > kc) | (kc >= page), 1.0, 0.0).astype(BF16)

    tail = tail_scr[...]
    for j in range(n_pg):
        hi, mid, lo = _split3(lf_refs[j][...])
        full = _dot(hi, after) + _dot(mid, after) + _dot(lo, after)
        b_scr[:, j * page:(j + 1) * page] = full[:, 0:page] + tail
        tail = tail + full[:, page:2 * page]
    tail_scr[...] = tail

    def scores(h, carry):
        h8 = pl.ds(pl.multiple_of(h * n_new, n_new), n_new)
        q_h = q_scr[h8, :].astype(BF16)
        k_h = jnp.concatenate([k_refs[j][h] for j in range(n_pg)], axis=1).astype(BF16)
        s_scr[h8, :] = _dot(q_h, k_h) + b_scr[pl.ds(h, 1), :]
        return carry

    lax.fori_loop(0, n_heads, scores, 0, unroll=head_unroll)

    cnew = cnew_scr[...]
    s_all = [s_scr[:, j * page:(j + 1) * page] + cnew for j in range(n_pg)]
    m_chunk = s_all[0]
    for j in range(1, n_pg):
        m_chunk = jnp.maximum(m_chunk, s_all[j])
    m_old = m_scr[...]
    m_new = jnp.maximum(m_old, jnp.max(m_chunk, axis=-1, keepdims=True))
    alpha = jnp.exp(m_old - m_new)
    m_scr[...] = m_new
    alpha_scr[...] = alpha
    l_new = alpha * l_scr[...]
    for j in range(n_pg):
        p = jnp.exp(s_all[j] - m_new)
        l_new = l_new + p
        p_scr[:, j * page:(j + 1) * page] = p
    l_scr[...] = l_new

    def values(h, carry):
        h8 = pl.ds(pl.multiple_of(h * n_new, n_new), n_new)
        v_h = jnp.concatenate([v_refs[j][h] for j in range(n_pg)], axis=1).astype(BF16)
        contrib = _dot_nt(p_scr[h8, :].astype(BF16), v_h)
        acc_scr[h8, :] = alpha_scr[h8, 0:head_dim] * acc_scr[h8, :] + contrib
        return carry

    lax.fori_loop(0, n_heads, values, 0, unroll=head_unroll)

    @pl.when(step == n_steps - 1)
    def _():
        den = jnp.sum(l_scr[...], axis=-1, keepdims=True)
        res = acc_scr[...] / den
        for h in range(n_heads):
            o_ref[:, h * head_dim:(h + 1) * head_dim] = res[h * n_new:(h + 1) * n_new, :]


def _attn_decode(qkv_s, logf_s, cache_k, cache_v, cache_logf, page_table, layer, n_pg):
    n_layers, n_pool, page, n_heads, head_dim = cache_k.shape
    db, n_pages = page_table.shape
    n_new = qkv_s.shape[0] // db
    d = n_heads * head_dim
    rows = n_heads * n_new
    n_steps = n_pages // n_pg
    ck = cache_k.transpose(0, 1, 3, 4, 2)
    cv = cache_v.transpose(0, 1, 3, 4, 2)
    clf = cache_logf.transpose(0, 1, 3, 2)

    def page_of(b, s, pt, j):
        return pt[b * n_pages + (n_pages - 1 - (s * n_pg + j))]

    kv_spec = lambda j: pl.BlockSpec((None, None, n_heads, head_dim, page),
                                     lambda b, s, pt: (layer, page_of(b, s, pt, j), 0, 0, 0))
    lf_spec = lambda j: pl.BlockSpec((None, None, n_heads, page),
                                     lambda b, s, pt: (layer, page_of(b, s, pt, j), 0, 0))
    new_spec = lambda c: pl.BlockSpec((n_new, d), lambda b, s, pt: (b, c))
    in_specs = [new_spec(0), new_spec(1), new_spec(2),
                pl.BlockSpec((n_new, n_heads), lambda b, s, pt: (b, 0))]
    in_specs += [kv_spec(j) for j in range(n_pg)] + [kv_spec(j) for j in range(n_pg)]
    in_specs += [lf_spec(j) for j in range(n_pg)]
    kern = functools.partial(_attn_decode_kernel, n_pg=n_pg, n_heads=n_heads, head_dim=head_dim,
                             n_new=n_new, page=page, n_steps=n_steps)
    stat = pltpu.VMEM((rows, LANES), F32)
    grid_spec = pltpu.PrefetchScalarGridSpec(
        num_scalar_prefetch=1,
        grid=(db, n_steps),
        in_specs=in_specs,
        out_specs=pl.BlockSpec((n_new, d), lambda b, s, pt: (b, 0)),
        scratch_shapes=[pltpu.VMEM((rows, head_dim), F32),
                        pltpu.VMEM((rows, n_pg * page), F32),
                        pltpu.VMEM((rows, n_pg * page), F32),
                        pltpu.VMEM((rows, head_dim), F32),
                        stat, stat, stat, stat,
                        pltpu.VMEM((n_heads, LANES), F32),
                        pltpu.VMEM((n_heads, n_pg * page), F32)],
    )
    return pl.pallas_call(
        kern,
        grid_spec=grid_spec,
        out_shape=jax.ShapeDtypeStruct((db * n_new, d), F32),
        compiler_params=_params(56, ("parallel", "arbitrary")),
        name="attn_decode",
    )(page_table.reshape(-1), qkv_s, qkv_s, qkv_s, logf_s,
      *([ck] * n_pg), *([cv] * n_pg), *([clf] * n_pg))


def _s5_kernel(u_ref, lre_ref, lim_ref, ldt_ref, bre_ref, bim_ref, cre_ref, cim_ref, d_ref,
               x0r_ref, x0i_ref, y_ref, fr_ref, fi_ref,
               w_scr, c_scr, ar_scr, ai_scr, xs_scr, sr_scr, si_scr, *, nb, n_tc, sp):
    tc = pl.program_id(1)
    rows = xs_scr.shape[1]
    steps = rows // nb
    nct = sp // LANES

    def load_cols(row_sel, first, count):
        return jnp.concatenate([xs_scr[first + j, row_sel, :] for j in range(count)], axis=1)

    def store_cols(row_sel, first, val):
        for j in range(val.shape[1] // LANES):
            xs_scr[first + j, row_sel, :] = val[:, j * LANES:(j + 1) * LANES]

    @pl.when(tc == 0)
    def _():
        lre = lre_ref[...]
        lim = lim_ref[...]
        dt = jnp.exp(ldt_ref[...])
        mag = jnp.exp(lre * dt)
        ar = mag * jnp.cos(lim * dt)
        ai = mag * jnp.sin(lim * dt)
        ar_scr[...] = ar
        ai_scr[...] = ai
        xr = ar - 1.0
        den = lre * lre + lim * lim
        cr = (xr * lre + ai * lim) / den
        ci = (ai * lre - xr * lim) / den
        uc, p_dim = bre_ref.shape
        c_dim = cre_ref.shape[1]

        def block_diag(x_ref, out_rows, out_cols, row_blk, col_blk):
            tr = lax.broadcasted_iota(jnp.int32, (col_blk, out_cols), 0)
            tcol = lax.broadcasted_iota(jnp.int32, (col_blk, out_cols), 1)
            rep = jnp.where((tcol & (col_blk - 1)) == tr, 1.0, 0.0).astype(BF16)
            hi, mid, lo = _split3(x_ref[...])
            full = _dot(hi, rep) + _dot(mid, rep) + _dot(lo, rep)
            rr = lax.broadcasted_iota(jnp.int32, (out_rows, out_cols), 0)
            cc = lax.broadcasted_iota(jnp.int32, (out_rows, out_cols), 1)
            same = (rr >> (row_blk.bit_length() - 1)) == (cc >> (col_blk.bit_length() - 1))
            return jnp.where(same, full, 0.0)

        bre = block_diag(bre_ref, uc, sp, c_dim, p_dim)
        bim = block_diag(bim_ref, uc, sp, c_dim, p_dim)
        w_scr[:, 0:sp] = (bre * cr - bim * ci).astype(BF16)
        w_scr[:, sp:2 * sp] = (bre * ci + bim * cr).astype(BF16)
        c_scr[0:sp, :] = block_diag(cre_ref, sp, uc, p_dim, c_dim).astype(BF16)
        c_scr[sp:2 * sp, :] = (-block_diag(cim_ref, sp, uc, p_dim, c_dim)).astype(BF16)
        sr_scr[...] = x0r_ref[...]
        si_scr[...] = x0i_ref[...]

    for b in range(nb):
        store_cols(pl.ds(b, steps, stride=nb), 0, _dot(u_ref[b].astype(BF16), w_scr[...]))
    ar8 = jnp.broadcast_to(ar_scr[...], (8, sp))
    ai8 = jnp.broadcast_to(ai_scr[...], (8, sp))
    low = lax.broadcasted_iota(jnp.int32, (8, sp), 0) < 4

    def body(s, carry):
        pr, pi = carry
        r0 = pl.ds(pl.multiple_of(s * 8, 8), 8)
        br = load_cols(r0, 0, nct)
        bi = load_cols(r0, nct, nct)
        er = ar8 * pr - ai8 * pi + br
        ei = ar8 * pi + ai8 * pr + bi
        if nb == 8:
            store_cols(r0, 0, er)
            store_cols(r0, nct, ei)
            return er, ei
        er4 = pltpu.roll(er, 4, 0)
        ei4 = pltpu.roll(ei, 4, 0)
        orr = ar8 * er4 - ai8 * ei4 + br
        oi = ar8 * ei4 + ai8 * er4 + bi
        store_cols(r0, 0, jnp.where(low, er, orr))
        store_cols(r0, nct, jnp.where(low, ei, oi))
        return pltpu.roll(orr, 4, 0), pltpu.roll(oi, 4, 0)

    fr, fi = lax.fori_loop(0, rows // 8, body, (sr_scr[...], si_scr[...]))
    sr_scr[...] = fr
    si_scr[...] = fi
    for b in range(nb):
        x_b = load_cols(pl.ds(b, steps, stride=nb), 0, 2 * nct).astype(BF16)
        y = _dot(x_b, c_scr[...]) + d_ref[...] * u_ref[b]
        y_ref[b] = jax.nn.gelu(y).astype(y_ref.dtype)

    @pl.when(tc == n_tc - 1)
    def _():
        fr_ref[...] = fr
        fi_ref[...] = fi


def _s5_layout(a_re, a_im, b_re, b_im, c_re, c_im, log_dt, gc):
    g, p = a_re.shape
    c = b_re.shape[-1]
    nch = g // gc
    assert p & (p - 1) == 0 and c & (c - 1) == 0, "block masks use shifts"
    row = lambda x: x.reshape(nch, 1, gc * p)
    b_rows = lambda b: b.transpose(0, 2, 1).reshape(nch, gc * c, p)
    c_rows = lambda cm: cm.transpose(0, 2, 1).reshape(nch, gc * p, c)
    return (row(a_re), row(a_im), row(jnp.repeat(log_dt, p)),
            b_rows(b_re), b_rows(b_im), c_rows(c_re), c_rows(c_im))


def _s5_scan(u, layout, d_skip, x0_re, x0_im, nb, steps):
    lre, lim, ldt, bre, bim, cre, cim = layout
    rows_total, d = u.shape
    nch, uc, p_dim = bre.shape
    sp, c_dim = cre.shape[1:]
    t_len = rows_total // nb
    n_tc = t_len // steps
    assert t_len % steps == 0 and (steps * nb) % 8 == 0 and nb in (4, 8)
    kern = functools.partial(_s5_kernel, nb=nb, n_tc=n_tc, sp=sp)
    par = lambda r, c: pl.BlockSpec((None, r, c), lambda g, t: (g, 0, 0))
    st_spec = pl.BlockSpec((8, sp), lambda g, t: (0, g))
    seq_spec = pl.BlockSpec((nb, steps, uc), lambda g, t: (0, t, g))
    y, f_re, f_im = pl.pallas_call(
        kern,
        grid=(nch, n_tc),
        in_specs=[seq_spec, par(1, sp), par(1, sp), par(1, sp),
                  par(uc, p_dim), par(uc, p_dim), par(sp, c_dim), par(sp, c_dim),
                  pl.BlockSpec((1, uc), lambda g, t: (0, g)),
                  st_spec, st_spec],
        out_specs=[seq_spec, st_spec, st_spec],
        out_shape=[jax.ShapeDtypeStruct((nb, t_len, d), BF16),
                   jax.ShapeDtypeStruct(x0_re.shape, F32), jax.ShapeDtypeStruct(x0_re.shape, F32)],
        scratch_shapes=[pltpu.VMEM((uc, 2 * sp), BF16), pltpu.VMEM((2 * sp, uc), BF16),
                        pltpu.VMEM((1, sp), F32), pltpu.VMEM((1, sp), F32),
                        pltpu.VMEM((2 * sp // LANES, steps * nb, LANES), F32),
                        pltpu.VMEM((8, sp), F32), pltpu.VMEM((8, sp), F32)],
        compiler_params=_params(56, ("parallel", "arbitrary")),
        name="s5_scan",
    )(u.reshape(nb, t_len, d), lre, lim, ldt, bre, bim, cre, cim, d_skip.reshape(1, d), x0_re, x0_im)
    return y.reshape(rows_total, d), f_re, f_im


def _row_tile(m, cap):
    best = None
    for t in range(16, cap + 1, 16):
        if m % t == 0:
            best = t
    return best if best is not None else m


def _col_tile(n, cap):
    best = LANES
    for t in range(LANES, cap + 1, LANES):
        if n % t == 0:
            best = t
    return best


def kernel(x_prompt, x_sample, cache_k, cache_v, cache_logf, state_s5_re, state_s5_im, page_table,
           meta_tokens, norm_g, ffn_w_up, ffn_w_down, attn_w_in, attn_b_f, attn_w_out,
           s5_A_re, s5_A_im, s5_B_re, s5_B_im, s5_C_re, s5_C_im, s5_log_dt, s5_D, s5_w_glu):
    bsz, seq, d = x_prompt.shape
    db, ds, _ = x_sample.shape
    n_meta = meta_tokens.shape[0]
    t_len = n_meta + seq
    mp, ms = bsz * t_len, db * ds
    depth = norm_g.shape[0]
    n_heads = attn_b_f.shape[-1]
    n_groups, state_dim = s5_A_re.shape[1:]

    tm = _row_tile(t_len, 704)
    tn_qkv = _col_tile(3 * d, 1024)
    tn_out = _col_tile(d, 512)
    tq = 256 if seq % 256 == 0 else LANES
    n_pages = page_table.shape[1]
    pages_per_step = max(p for p in (8, 4, 2, 1) if n_pages % p == 0)
    n_attn = (depth + 1) // 2

    w_in = attn_w_in.astype(BF16)
    w_out = attn_w_out.astype(BF16)
    w_glu = s5_w_glu.astype(BF16)

    meta = jnp.broadcast_to(meta_tokens[None].astype(x_prompt.dtype), (bsz, n_meta, d))
    hp = jnp.concatenate([meta, x_prompt], axis=1).reshape(mp, d)
    hs = x_sample.reshape(ms, d)

    lf_pr, sre_pr, sim_pr = [], [], []
    k_sa, v_sa, lf_sa, sre_sa, sim_sa = [], [], [], [], []
    kv_t = None
    up = _rmsnorm(hp, norm_g[0, 0], tm)
    us = _rmsnorm(hs, norm_g[0, 0], ms)
    for i in range(depth):
        g = norm_g[i]
        attn_layer = i % 2 == 0
        act_p, act_s, w_down = _ffn_up(up, us, ffn_w_up, ffn_w_down, i, 0, tm)
        hp, up, hs, us = _ffn_down(act_p, act_s, w_down, hp, hs, g[1], g[2],
                                   BF16 if attn_layer else F32, tm)
        if attn_layer:
            li = i // 2
            qkv_p, logf_p = _qkv_proj(up, w_in, attn_b_f, li, tm, tn_qkv)
            qkv_s, logf_s = _qkv_proj(us, w_in, attn_b_f, li, ms, tn_qkv)
            cparts = _prompt_cumsum(logf_p, bsz, t_len)
            yp, kv_t = _attn_prompt(qkv_p, cparts, bsz, t_len, n_meta, n_heads, tq, li, n_attn, kv_t)
            ys = _attn_decode(qkv_s, logf_s, cache_k, cache_v, cache_logf, page_table, li,
                              pages_per_step)
            hp, up, hs, us = _mixer_out(yp, ys, w_out, li, False, hp, hs, g[3], g[4], tm, tn_out)
            lf_pr.append(logf_p)
            k_sa.append(qkv_s[:, d:2 * d]); v_sa.append(qkv_s[:, 2 * d:]); lf_sa.append(logf_s)
        else:
            si = i // 2
            layout = _s5_layout(s5_A_re[si], s5_A_im[si], s5_B_re[si], s5_B_im[si],
                                s5_C_re[si], s5_C_im[si], s5_log_dt[si], 16)
            zeros = jnp.zeros((8, n_groups * state_dim), F32)
            yp, re_p, im_p = _s5_scan(up, layout, s5_D[si], zeros, zeros, bsz, tm)
            ys, re_s, im_s = _s5_scan(us, layout, s5_D[si],
                                      state_s5_re[si].reshape(db, -1), state_s5_im[si].reshape(db, -1),
                                      db, ds)
            hp, up, hs, us = _mixer_out(yp, ys, w_glu, si, True, hp, hs, g[3], g[4], tm, tn_out)
            sre_pr.append(re_p[:bsz]); sim_pr.append(im_p[:bsz])
            sre_sa.append(re_s[:db]); sim_sa.append(im_s[:db])
        act_p, act_s, w_down = _ffn_up(up, us, ffn_w_up, ffn_w_down, i, 1, tm)
        g_next = norm_g[i + 1, 0] if i + 1 < depth else None
        hp, up, hs, us = _ffn_down(act_p, act_s, w_down, hp, hs, g[5], g_next, BF16, tm)

    hd = d // n_heads
    stack = lambda xs, shape: jnp.stack(xs).reshape((len(xs),) + shape)
    kv_out = lambda x: x.reshape(n_attn, bsz, n_heads, hd, t_len).transpose(0, 1, 4, 2, 3)
    return (hp.reshape(bsz, t_len, d)[:, n_meta:], hs.reshape(db, ds, d),
            kv_out(kv_t[0]), kv_out(kv_t[1]),
            stack(lf_pr, (bsz, t_len, n_heads)),
            stack(sre_pr, (bsz, n_groups, state_dim)), stack(sim_pr, (bsz, n_groups, state_dim)),
            stack(k_sa, (db, ds, n_heads, hd)), stack(v_sa, (db, ds, n_heads, hd)),
            stack(lf_sa, (db, ds, n_heads)),
            stack(sre_sa, (db, n_groups, state_dim)), stack(sim_sa, (db, n_groups, state_dim)))
```

```python
import functools

import jax
import jax.numpy as jnp
from jax import lax
from jax.experimental import pallas as pl
from jax.experimental.pallas import tpu as pltpu

F32 = jnp.float32
BF16 = jnp.bfloat16
RMS_EPS = 1e-6
LOG2E = 1.4426950408889634
LANES = 128
MIB = 1024 * 1024
NT_DIMS = (((1,), (1,)), ((), ()))


def _params(vmem_mib, semantics=None):
    return pltpu.CompilerParams(dimension_semantics=semantics, vmem_limit_bytes=vmem_mib * MIB)


def _rms(x, g):
    ms = jnp.mean(x * x, axis=-1, keepdims=True)
    return (x * lax.rsqrt(ms + RMS_EPS)) * g


def _split3(x):
    hi = x.astype(BF16)
    r1 = x - hi.astype(F32)
    mid = r1.astype(BF16)
    lo = (r1 - mid.astype(F32)).astype(BF16)
    return hi, mid, lo


def _dot(a, b):
    return jnp.dot(a, b, preferred_element_type=F32)


def _dot_nt(a, b):
    return lax.dot_general(a, b, NT_DIMS, preferred_element_type=F32)


def _rmsnorm_kernel(h_ref, g_ref, o_ref):
    o_ref[...] = _rms(h_ref[...], g_ref[...]).astype(o_ref.dtype)


def _rmsnorm(h, g, tm):
    m, d = h.shape
    return pl.pallas_call(
        _rmsnorm_kernel,
        grid=(m // tm,),
        in_specs=[pl.BlockSpec((tm, d), lambda i: (i, 0)), pl.BlockSpec((1, d), lambda i: (0, 0))],
        out_specs=pl.BlockSpec((tm, d), lambda i: (i, 0)),
        out_shape=jax.ShapeDtypeStruct((m, d), BF16),
        compiler_params=_params(40),
        name="rmsnorm",
    )(h, g.reshape(1, d))


def _ffn_up_kernel(up_ref, us_ref, wa_ref, wb_ref, wd_ref, op_ref, os_ref, wdb_ref, w_scr,
                   *, n_chunks, ch):
    w_scr[:, :LANES] = wa_ref[...].astype(BF16)
    w_scr[:, LANES:] = wb_ref[...].astype(BF16)
    wdb_ref[...] = wd_ref[...].astype(BF16)

    def swiglu(x):
        z = _dot(x, w_scr[...])
        a = z[:, :LANES]
        b = z[:, LANES:]
        return (a * jax.nn.sigmoid(a) * b).astype(BF16)

    def body(c, carry):
        rows = pl.ds(pl.multiple_of(c * ch, ch), ch)
        op_ref[rows, :] = swiglu(up_ref[rows, :])
        return carry

    lax.fori_loop(0, n_chunks, body, 0, unroll=True)
    os_ref[...] = swiglu(us_ref[...])


def _ffn_up(up, us, w_up, w_down, layer, half, ch):
    mp, d = up.shape
    ms = us.shape[0]
    f = w_up.shape[-1] // 2
    nt = f // LANES
    kern = functools.partial(_ffn_up_kernel, n_chunks=mp // ch, ch=ch)
    return pl.pallas_call(
        kern,
        grid=(nt,),
        in_specs=[
            pl.BlockSpec(memory_space=pltpu.VMEM),
            pl.BlockSpec(memory_space=pltpu.VMEM),
            pl.BlockSpec((None, None, d, LANES), lambda j: (layer, half, 0, j)),
            pl.BlockSpec((None, None, d, LANES), lambda j: (layer, half, 0, j + nt)),
            pl.BlockSpec((None, None, LANES, d), lambda j: (layer, half, j, 0)),
        ],
        out_specs=[pl.BlockSpec((mp, LANES), lambda j: (0, j)),
                   pl.BlockSpec((ms, LANES), lambda j: (0, j)),
                   pl.BlockSpec((LANES, d), lambda j: (j, 0))],
        out_shape=[jax.ShapeDtypeStruct((mp, f), BF16), jax.ShapeDtypeStruct((ms, f), BF16),
                   jax.ShapeDtypeStruct((f, d), BF16)],
        scratch_shapes=[pltpu.VMEM((d, 2 * LANES), BF16)],
        compiler_params=_params(58),
        name="ffn_up",
    )(up, us, w_up, w_up, w_down)


def _residual_epilogue(y, h_ref, gp_ref, gn_ref, hn_ref, un_ref, scale):
    hn = h_ref[...] + scale * _rms(y, gp_ref[...])
    hn_ref[...] = hn
    if un_ref is not None:
        un_ref[...] = _rms(hn, gn_ref[...]).astype(un_ref.dtype)


def _down_tiling(f):
    nt = f // LANES
    for dmul in (8, 6, 4, 2, 7, 5, 3):
        if nt % dmul == 0 and nt // dmul >= 1:
            return dmul * LANES, nt // dmul, False
    for dmul in (8, 6, 4, 2, 7, 5, 3):
        if (nt - 1) % dmul == 0:
            return dmul * LANES, (nt - 1) // dmul, True
    return LANES, nt, False


class _RowRefs:
    def __init__(self, x, xt, h, hn, un, acc):
        self.x, self.xt, self.h, self.hn, self.un, self.acc = x, xt, h, hn, un, acc


def _unpack_rows(refs, has_tail, has_next):
    refs = list(refs)
    take = lambda cond=True: refs.pop(0) if cond else None
    xp, xtp, xs, xts = take(), take(has_tail), take(), take(has_tail)
    return refs, xp, xtp, xs, xts


def _ffn_down_kernel(*refs, nk, scale, has_tail, has_next):
    refs, xp, xtp, xs, xts = _unpack_rows(refs, has_tail, has_next)
    take = lambda cond=True: refs.pop(0) if cond else None
    wm_ref, wt_ref = take(), take(has_tail)
    hp, hs, gp_ref, gn_ref = take(), take(), take(), take(has_next)
    hnp, unp, hns, uns = take(), take(has_next), take(), take(has_next)
    accp, accs = take(), take()
    prompt = _RowRefs(xp, xtp, hp, hnp, unp, accp)
    sample = _RowRefs(xs, xts, hs, hns, uns, accs)
    i = pl.program_id(0)
    k = pl.program_id(1)

    def accumulate(r):
        @pl.when(k == 0)
        def _():
            if has_tail:
                r.acc[...] = _dot(r.xt[...], wt_ref[...])
            else:
                r.acc[...] = jnp.zeros(r.acc.shape, F32)

        r.acc[...] += _dot(r.x[...], wm_ref[...])

        @pl.when(k == nk - 1)
        def _():
            _residual_epilogue(r.acc[...], r.h, gp_ref, gn_ref, r.hn, r.un, scale)

    accumulate(prompt)
    pl.when(i == 0)(lambda: accumulate(sample))


def _ffn_down(act_p, act_s, w_down, h_p, h_s, g_post, g_next, u_dtype, tm):
    mp, f = act_p.shape
    ms = act_s.shape[0]
    d = h_p.shape[1]
    tk, nk, has_tail = _down_tiling(f)
    has_next = g_next is not None
    tail_blk = (f - LANES) // LANES
    in_specs = [pl.BlockSpec((tm, tk), lambda i, k: (i, k))]
    args = [act_p]
    if has_tail:
        in_specs.append(pl.BlockSpec((tm, LANES), lambda i, k: (i, tail_blk)))
        args.append(act_p)
    in_specs.append(pl.BlockSpec((ms, tk), lambda i, k: (0, k)))
    args.append(act_s)
    if has_tail:
        in_specs.append(pl.BlockSpec((ms, LANES), lambda i, k: (0, tail_blk)))
        args.append(act_s)
    in_specs.append(pl.BlockSpec((tk, d), lambda i, k: (k, 0)))
    args.append(w_down)
    if has_tail:
        in_specs.append(pl.BlockSpec((LANES, d), lambda i, k: (tail_blk, 0)))
        args.append(w_down)
    row_spec = pl.BlockSpec((tm, d), lambda i, k: (i, 0))
    srow_spec = pl.BlockSpec((ms, d), lambda i, k: (0, 0))
    vec_spec = pl.BlockSpec((1, d), lambda i, k: (0, 0))
    in_specs += [row_spec, srow_spec, vec_spec]
    args += [h_p, h_s, g_post.reshape(1, d)]
    out_specs = [row_spec]
    out_shape = [jax.ShapeDtypeStruct((mp, d), F32)]
    if has_next:
        in_specs.append(vec_spec)
        args.append(g_next.reshape(1, d))
        out_specs.append(row_spec)
        out_shape.append(jax.ShapeDtypeStruct((mp, d), u_dtype))
    out_specs.append(srow_spec)
    out_shape.append(jax.ShapeDtypeStruct((ms, d), F32))
    if has_next:
        out_specs.append(srow_spec)
        out_shape.append(jax.ShapeDtypeStruct((ms, d), u_dtype))
    kern = functools.partial(_ffn_down_kernel, nk=nk, scale=0.5, has_tail=has_tail, has_next=has_next)
    res = pl.pallas_call(
        kern,
        grid=(mp // tm, nk),
        in_specs=in_specs,
        out_specs=out_specs,
        out_shape=out_shape,
        scratch_shapes=[pltpu.VMEM((tm, d), F32), pltpu.VMEM((ms, d), F32)],
        compiler_params=_params(56, ("arbitrary", "arbitrary")),
        name="ffn_down",
    )(*args)
    if has_next:
        return res[0], res[1], res[2], res[3]
    return res[0], None, res[1], None


def _mixer_out_kernel(xp, xs, *refs, nn, tn, glu):
    refs = list(refs)
    take = lambda cond=True: refs.pop(0) if cond else None
    wa_ref, wb_ref = take(), take(glu)
    hp, hs, gp_ref, gn_ref = take(), take(), take(), take()
    hnp, unp, hns, uns = take(), take(), take(), take()
    accp, accs = take(), take()
    prompt = _RowRefs(xp, None, hp, hnp, unp, accp)
    sample = _RowRefs(xs, None, hs, hns, uns, accs)
    i = pl.program_id(0)
    j = pl.program_id(1)
    col = pl.multiple_of(j * tn, tn)

    def project(r):
        x = r.x[...].astype(BF16)
        if glu:
            y = _dot(x, wa_ref[...]) * jax.nn.sigmoid(_dot(x, wb_ref[...]))
        else:
            y = _dot(x, wa_ref[...])
        r.acc[:, pl.ds(col, tn)] = y

        @pl.when(j == nn - 1)
        def _():
            _residual_epilogue(r.acc[...], r.h, gp_ref, gn_ref, r.hn, r.un, 1.0)

    project(prompt)
    pl.when(i == 0)(lambda: project(sample))


def _mixer_out(x_p, x_s, w, layer, glu, h_p, h_s, g_post, g_next, tm, tn):
    mp, d = h_p.shape
    ms = h_s.shape[0]
    kdim = x_s.shape[1]
    nn = d // tn
    row_spec = pl.BlockSpec((tm, d), lambda i, j: (i, 0))
    srow_spec = pl.BlockSpec((ms, d), lambda i, j: (0, 0))
    vec_spec = pl.BlockSpec((1, d), lambda i, j: (0, 0))
    in_specs = [pl.BlockSpec((tm, kdim), lambda i, j: (i, 0)),
                pl.BlockSpec((ms, kdim), lambda i, j: (0, 0)),
                pl.BlockSpec((None, kdim, tn), lambda i, j: (layer, 0, j))]
    args = [x_p, x_s, w]
    if glu:
        in_specs.append(pl.BlockSpec((None, kdim, tn), lambda i, j: (layer, 0, j + nn)))
        args.append(w)
    in_specs += [row_spec, srow_spec, vec_spec, vec_spec]
    args += [h_p, h_s, g_post.reshape(1, d), g_next.reshape(1, d)]
    return pl.pallas_call(
        functools.partial(_mixer_out_kernel, nn=nn, tn=tn, glu=glu),
        grid=(mp // tm, nn),
        in_specs=in_specs,
        out_specs=[row_spec, row_spec, srow_spec, srow_spec],
        out_shape=[jax.ShapeDtypeStruct((mp, d), F32), jax.ShapeDtypeStruct((mp, d), BF16),
                   jax.ShapeDtypeStruct((ms, d), F32), jax.ShapeDtypeStruct((ms, d), BF16)],
        scratch_shapes=[pltpu.VMEM((tm, d), F32), pltpu.VMEM((ms, d), F32)],
        compiler_params=_params(56, ("arbitrary", "arbitrary")),
        name="mixer_out_glu" if glu else "mixer_out",
    )(*args)


def _log_sigmoid(x):
    return jnp.minimum(x, 0.0) - jnp.log1p(jnp.exp(-jnp.abs(x)))


def _qkv_kernel(x_ref, w_ref, wf_ref, bf_ref, o_ref, lf_ref, *, n_heads):
    j = pl.program_id(1)
    x = x_ref[...]
    o_ref[...] = _dot(x, w_ref[...])

    @pl.when(j == 0)
    def _():
        f = _dot(x, wf_ref[...])[:, :n_heads] + bf_ref[...]
        lf_ref[...] = _log_sigmoid(f)


def _qkv_proj(u, w_in, b_f, layer, tm, tn):
    m, d = u.shape
    n_heads = b_f.shape[-1]
    n_main = 3 * d
    return pl.pallas_call(
        functools.partial(_qkv_kernel, n_heads=n_heads),
        grid=(m // tm, n_main // tn),
        in_specs=[
            pl.BlockSpec((tm, d), lambda i, j: (i, 0)),
            pl.BlockSpec((None, d, tn), lambda i, j: (layer, 0, j)),
            pl.BlockSpec((None, d, LANES), lambda i, j: (layer, 0, n_main // LANES)),
            pl.BlockSpec((None, 1, n_heads), lambda i, j: (layer, 0, 0)),
        ],
        out_specs=[pl.BlockSpec((tm, tn), lambda i, j: (i, j)),
                   pl.BlockSpec((tm, n_heads), lambda i, j: (i, 0))],
        out_shape=[jax.ShapeDtypeStruct((m, n_main), F32), jax.ShapeDtypeStruct((m, n_heads), F32)],
        compiler_params=_params(48, ("parallel", "arbitrary")),
        name="qkv_proj",
    )(u, w_in, w_in, b_f.reshape(b_f.shape[0], 1, n_heads))


def _cumsum_kernel(lf_ref, o_ref, *, t_len, chunk, n_heads):
    r = lax.broadcasted_iota(jnp.int32, (chunk, chunk), 0)
    c = lax.broadcasted_iota(jnp.int32, (chunk, chunk), 1)
    tri = jnp.where(c <= r, 1.0, 0.0).astype(BF16)
    carry = jnp.zeros((1, n_heads), F32)
    for s in range(t_len // chunk):
        x = lf_ref[s * chunk:(s + 1) * chunk, :]
        hi, mid, lo = _split3(x)
        cs = _dot(tri, hi) + _dot(tri, mid) + _dot(tri, lo) + carry
        carry = cs[chunk - 1:chunk, :]
        chi, cmid, clo = _split3(cs * LOG2E)
        o_ref[s * chunk:(s + 1) * chunk, 0:n_heads] = chi
        o_ref[s * chunk:(s + 1) * chunk, n_heads:2 * n_heads] = cmid
        o_ref[s * chunk:(s + 1) * chunk, 2 * n_heads:3 * n_heads] = clo


def _cumsum_chunk(t_len):
    for c in (688, 256, 128, 64, 48, 16):
        if t_len % c == 0:
            return c
    return t_len


def _prompt_cumsum(logf, batch, t_len):
    n_heads = logf.shape[1]
    kern = functools.partial(_cumsum_kernel, t_len=t_len, chunk=_cumsum_chunk(t_len), n_heads=n_heads)
    return pl.pallas_call(
        kern,
        grid=(batch,),
        in_specs=[pl.BlockSpec((t_len, n_heads), lambda b: (b, 0))],
        out_specs=pl.BlockSpec((t_len, 3 * n_heads), lambda b: (b, 0)),
        out_shape=jax.ShapeDtypeStruct((batch * t_len, 3 * n_heads), BF16),
        compiler_params=_params(32, ("parallel",)),
        name="prompt_cumsum",
    )(logf)


def _store_transposed(src_ref, dst_ref, t_len):
    n_full = t_len // LANES
    for c in range(n_full):
        dst_ref[:, c * LANES:(c + 1) * LANES] = src_ref[c * LANES:(c + 1) * LANES, :].T
    rem = t_len - n_full * LANES
    if rem:
        last = src_ref[t_len - LANES:t_len, :].T
        dst_ref[:, n_full * LANES:t_len] = last[:, LANES - rem:]


def _attn_prompt_kernel(q_ref, k_ref, v_ref, cp_ref, *rest, t_len, n_meta, tq, n_heads, head_dim,
                        aliased):
    if aliased:
        rest = rest[2:]
    o_ref, kt_ref, vt_ref, qa_scr, ka_scr, v_scr = rest
    _store_transposed(k_ref, kt_ref, t_len)
    _store_transposed(v_ref, vt_ref, t_len)
    hp = pl.program_id(1)
    half = LANES // 2
    lane = lax.broadcasted_iota(jnp.int32, (1, LANES), 1)
    r3 = lax.broadcasted_iota(jnp.int32, (3 * n_heads, LANES), 0)
    l3 = lax.broadcasted_iota(jnp.int32, (3 * n_heads, LANES), 1)
    cp = cp_ref[...]
    q = (q_ref[...] * (head_dim ** -0.5 * LOG2E)).astype(BF16)
    k = k_ref[...].astype(BF16)
    v_scr[...] = v_ref[...].astype(BF16)
    for hh in range(2):
        head = 2 * hp + hh
        base = half * (1 - hh)
        sel_q = jnp.zeros((3 * n_heads, LANES), F32)
        sel_k = jnp.zeros((3 * n_heads, LANES), F32)
        for part in range(3):
            row_hit = r3 == head + part * n_heads
            sel_q = sel_q + jnp.where(row_hit & (l3 == base + part), 1.0, 0.0)
            sel_k = sel_k + jnp.where(row_hit & (l3 == base + 3 + part), -1.0, 0.0)
        ones_q = jnp.where((lane >= base + 3) & (lane < base + 6), 1.0, 0.0)
        ones_k = jnp.where((lane >= base) & (lane < base + 3), 1.0, 0.0)
        ex_q = (_dot(cp, sel_q.astype(BF16)) + ones_q).astype(BF16)
        ex_k = (_dot(cp, sel_k.astype(BF16)) + ones_k).astype(BF16)
        data = (lane >= half * hh) & (lane < half * hh + half)
        qa_scr[hh] = jnp.where(data, q, ex_q)
        ka_scr[hh] = jnp.where(data, k, ex_k)

    tiles = [(0, n_meta)] + [(n_meta + i * tq, tq) for i in range((t_len - n_meta) // tq)]
    units = [(q0, tl, hh) for q0, tl in tiles for hh in range(2)]

    def scores(q0, tl, hh):
        qt = qa_scr[hh, q0:q0 + tl, :]
        row = lax.broadcasted_iota(jnp.int32, (tl, tl), 0)
        col = lax.broadcasted_iota(jnp.int32, (tl, tl), 1)
        sd = jnp.where(col <= row, _dot_nt(qt, ka_scr[hh, q0:q0 + tl, :]), -jnp.inf)
        so = _dot_nt(qt, ka_scr[hh, 0:q0, :]) if q0 > 0 else None
        return sd, so

    def softmax(sd, so):
        m = jnp.max(sd, axis=-1, keepdims=True)
        if so is not None:
            m = jnp.maximum(m, jnp.max(so, axis=-1, keepdims=True))
        pd = jnp.exp2(sd - m)
        den = jnp.sum(pd, axis=-1, keepdims=True)
        po = None
        if so is not None:
            po = jnp.exp2(so - m)
            den = den + jnp.sum(po, axis=-1, keepdims=True)
            po = po.astype(BF16)
        return pd.astype(BF16), po, den

    def values(q0, tl, pd, po, den):
        o = _dot(pd, v_scr[q0:q0 + tl, :])
        if po is not None:
            o = o + _dot(po, v_scr[0:q0, :])
        return o / den

    staged_s, staged_p, done = {}, {}, {}
    for k in range(len(units) + 2):
        if k < len(units):
            staged_s[k] = scores(*units[k])
        if 0 <= k - 1 < len(units):
            staged_p[k - 1] = softmax(*staged_s.pop(k - 1))
        if 0 <= k - 2 < len(units):
            q0, tl, hh = units[k - 2]
            done[hh] = values(q0, tl, *staged_p.pop(k - 2))
            if hh == 1:
                o_ref[q0:q0 + tl, :] = jnp.where(lane < half, done[0], done[1]).astype(o_ref.dtype)


def _attn_prompt(qkv, cparts, batch, t_len, n_meta, n_heads, tq, layer, n_layers, kv_t):
    d = qkv.shape[1] // 3
    head_dim = d // n_heads
    assert 2 * head_dim == LANES, "head pairs must fill one lane tile"
    assert (t_len - n_meta) % tq == 0 and t_len >= LANES
    npair = n_heads // 2
    aliased = kv_t is not None
    kern = functools.partial(_attn_prompt_kernel, t_len=t_len, n_meta=n_meta, tq=tq,
                             n_heads=n_heads, head_dim=head_dim, aliased=aliased)
    blk = lambda off: pl.BlockSpec((t_len, LANES), lambda b, p: (b, off + p))
    in_specs = [blk(0), blk(npair), blk(2 * npair),
                pl.BlockSpec((t_len, 3 * n_heads), lambda b, p: (b, 0))]
    args = [qkv, qkv, qkv, cparts]
    aliases = {}
    if aliased:
        in_specs += [pl.BlockSpec(memory_space=pl.ANY), pl.BlockSpec(memory_space=pl.ANY)]
        args += list(kv_t)
        aliases = {4: 1, 5: 2}
    t_spec = pl.BlockSpec((None, None, LANES, t_len), lambda b, p: (layer, b, p, 0))
    t_shape = jax.ShapeDtypeStruct((n_layers, batch, d, t_len), F32)
    y, k_t, v_t = pl.pallas_call(
        kern,
        grid=(batch, npair),
        in_specs=in_specs,
        out_specs=[pl.BlockSpec((t_len, LANES), lambda b, p: (b, p)), t_spec, t_spec],
        out_shape=[jax.ShapeDtypeStruct((batch * t_len, d), BF16), t_shape, t_shape],
        scratch_shapes=[pltpu.VMEM((2, t_len, LANES), BF16), pltpu.VMEM((2, t_len, LANES), BF16),
                        pltpu.VMEM((t_len, LANES), BF16)],
        input_output_aliases=aliases,
        compiler_params=_params(48, ("parallel", "arbitrary")),
        name="attn_prompt",
    )(*args)
    return y, (k_t, v_t)


def _attn_decode_kernel(pt_ref, q_ref, kn_ref, vn_ref, lfn_ref, *rest,
                        n_pg, n_heads, head_dim, n_new, page, n_steps):
    del pt_ref
    k_refs = rest[:n_pg]
    v_refs = rest[n_pg:2 * n_pg]
    lf_refs = rest[2 * n_pg:3 * n_pg]
    o_ref = rest[3 * n_pg]
    (q_scr, s_scr, p_scr, acc_scr, m_scr, l_scr, cnew_scr, alpha_scr,
     tail_scr, b_scr) = rest[3 * n_pg + 1:]
    step = pl.program_id(1)
    rows = n_heads * n_new
    head_unroll = 8 if n_heads % 8 == 0 else 1
    er = lax.broadcasted_iota(jnp.int32, (rows, n_heads), 0)
    ec = lax.broadcasted_iota(jnp.int32, (rows, n_heads), 1)
    expand = jnp.where((er >= ec * n_new) & (er < (ec + 1) * n_new), 1.0, 0.0).astype(BF16)

    def spread(lf):
        return [_dot_nt(expand, part).astype(BF16) for part in _split3(lf)]

    def head_update(h8, p_h, v_h, first):
        contrib = _dot(p_h, v_h)
        if first:
            acc_scr[h8, :] = contrib
        else:
            acc_scr[h8, :] = alpha_scr[h8, 0:head_dim] * acc_scr[h8, :] + contrib

    @pl.when(step == 0)
    def _():
        scale = head_dim ** -0.5
        for h in range(n_heads):
            q_scr[h * n_new:(h + 1) * n_new, :] = q_ref[:, h * head_dim:(h + 1) * head_dim] * scale
        kr = lax.broadcasted_iota(jnp.int32, (n_new, LANES), 0)
        kc = lax.broadcasted_iota(jnp.int32, (n_new, LANES), 1)
        upper = jnp.where(kr <= kc, 1.0, 0.0).astype(BF16)
        parts = spread(lfn_ref[...])
        cum = _dot(parts[0], upper) + _dot(parts[1], upper) + _dot(parts[2], upper)
        rq = lax.broadcasted_iota(jnp.int32, (rows, LANES), 0)
        lq = lax.broadcasted_iota(jnp.int32, (rows, LANES), 1)
        qpos = rq & (n_new - 1)
        c_q = jnp.sum(jnp.where(lq == qpos, cum, 0.0), axis=-1, keepdims=True)
        cnew_scr[...] = jnp.broadcast_to(c_q, (rows, LANES))
        bias = c_q - cum
        valid = lq <= qpos
        s_scr[:, 0:LANES] = jnp.zeros((rows, LANES), F32)
        for h in range(n_heads):
            h8 = pl.ds(h * n_new, n_new)
            k_h = kn_ref[:, h * head_dim:(h + 1) * head_dim].astype(BF16)
            s_scr[h8, 0:n_new] = _dot_nt(q_scr[h8, :].astype(BF16), k_h)
        s = jnp.where(valid, s_scr[:, 0:LANES] + bias, -jnp.inf)
        m = jnp.max(s, axis=-1, keepdims=True)
        p = jnp.exp(s - m)
        m_scr[...] = jnp.broadcast_to(m, (rows, LANES))
        l_scr[...] = p
        tail_scr[...] = jnp.zeros((n_heads, LANES), F32)
        p_scr[:, 0:LANES] = p
        for h in range(n_heads):
            h8 = pl.ds(h * n_new, n_new)
            v_h = vn_ref[:, h * head_dim:(h + 1) * head_dim].astype(BF16)
            head_update(h8, p_scr[h8, 0:n_new].astype(BF16), v_h, True)

    kr = lax.broadcasted_iota(jnp.int32, (page, 2 * page), 0)
    kc = lax.broadcasted_iota(jnp.int32, (page, 2 * page), 1)
    after = jnp.where((kr > kc) | (kc >= page), 1.0, 0.0).astype(BF16)

    tail = tail_scr[...]
    for j in range(n_pg):
        hi, mid, lo = _split3(lf_refs[j][...])
        full = _dot(hi, after) + _dot(mid, after) + _dot(lo, after)
        b_scr[:, j * page:(j + 1) * page] = full[:, 0:page] + tail
        tail = tail + full[:, page:2 * page]
    tail_scr[...] = tail

    def scores(h, carry):
        h8 = pl.ds(pl.multiple_of(h * n_new, n_new), n_new)
        q_h = q_scr[h8, :].astype(BF16)
        k_h = jnp.concatenate([k_refs[j][h] for j in range(n_pg)], axis=1).astype(BF16)
        s_scr[h8, :] = _dot(q_h, k_h) + b_scr[pl.ds(h, 1), :]
        return carry

    lax.fori_loop(0, n_heads, scores, 0, unroll=head_unroll)

    cnew = cnew_scr[...]
    s_all = [s_scr[:, j * page:(j + 1) * page] + cnew for j in range(n_pg)]
    m_chunk = s_all[0]
    for j in range(1, n_pg):
        m_chunk = jnp.maximum(m_chunk, s_all[j])
    m_old = m_scr[...]
    m_new = jnp.maximum(m_old, jnp.max(m_chunk, axis=-1, keepdims=True))
    alpha = jnp.exp(m_old - m_new)
    m_scr[...] = m_new
    alpha_scr[...] = alpha
    l_new = alpha * l_scr[...]
    for j in range(n_pg):
        p = jnp.exp(s_all[j] - m_new)
        l_new = l_new + p
        p_scr[:, j * page:(j + 1) * page] = p
    l_scr[...] = l_new

    def values(h, carry):
        h8 = pl.ds(pl.multiple_of(h * n_new, n_new), n_new)
        v_h = jnp.concatenate([v_refs[j][h] for j in range(n_pg)], axis=1).astype(BF16)
        contrib = _dot_nt(p_scr[h8, :].astype(BF16), v_h)
        acc_scr[h8, :] = alpha_scr[h8, 0:head_dim] * acc_scr[h8, :] + contrib
        return carry

    lax.fori_loop(0, n_heads, values, 0, unroll=head_unroll)

    @pl.when(step == n_steps - 1)
    def _():
        den = jnp.sum(l_scr[...], axis=-1, keepdims=True)
        res = acc_scr[...] / den
        for h in range(n_heads):
            o_ref[:, h * head_dim:(h + 1) * head_dim] = res[h * n_new:(h + 1) * n_new, :]


def _attn_decode(qkv_s, logf_s, cache_k, cache_v, cache_logf, page_table, layer, n_pg):
    n_layers, n_pool, page, n_heads, head_dim = cache_k.shape
    db, n_pages = page_table.shape
    n_new = qkv_s.shape[0] // db
    d = n_heads * head_dim
    rows = n_heads * n_new
    n_steps = n_pages // n_pg
    ck = cache_k.transpose(0, 1, 3, 4, 2)
    cv = cache_v.transpose(0, 1, 3, 4, 2)
    clf = cache_logf.transpose(0, 1, 3, 2)

    def page_of(b, s, pt, j):
        return pt[b * n_pages + (n_pages - 1 - (s * n_pg + j))]

    kv_spec = lambda j: pl.BlockSpec((None, None, n_heads, head_dim, page),
                                     lambda b, s, pt: (layer, page_of(b, s, pt, j), 0, 0, 0))
    lf_spec = lambda j: pl.BlockSpec((None, None, n_heads, page),
                                     lambda b, s, pt: (layer, page_of(b, s, pt, j), 0, 0))
    new_spec = lambda c: pl.BlockSpec((n_new, d), lambda b, s, pt: (b, c))
    in_specs = [new_spec(0), new_spec(1), new_spec(2),
                pl.BlockSpec((n_new, n_heads), lambda b, s, pt: (b, 0))]
    in_specs += [kv_spec(j) for j in range(n_pg)] + [kv_spec(j) for j in range(n_pg)]
    in_specs += [lf_spec(j) for j in range(n_pg)]
    kern = functools.partial(_attn_decode_kernel, n_pg=n_pg, n_heads=n_heads, head_dim=head_dim,
                             n_new=n_new, page=page, n_steps=n_steps)
    stat = pltpu.VMEM((rows, LANES), F32)
    grid_spec = pltpu.PrefetchScalarGridSpec(
        num_scalar_prefetch=1,
        grid=(db, n_steps),
        in_specs=in_specs,
        out_specs=pl.BlockSpec((n_new, d), lambda b, s, pt: (b, 0)),
        scratch_shapes=[pltpu.VMEM((rows, head_dim), F32),
                        pltpu.VMEM((rows, n_pg * page), F32),
                        pltpu.VMEM((rows, n_pg * page), F32),
                        pltpu.VMEM((rows, head_dim), F32),
                        stat, stat, stat, stat,
                        pltpu.VMEM((n_heads, LANES), F32),
                        pltpu.VMEM((n_heads, n_pg * page), F32)],
    )
    return pl.pallas_call(
        kern,
        grid_spec=grid_spec,
        out_shape=jax.ShapeDtypeStruct((db * n_new, d), F32),
        compiler_params=_params(56, ("parallel", "arbitrary")),
        name="attn_decode",
    )(page_table.reshape(-1), qkv_s, qkv_s, qkv_s, logf_s,
      *([ck] * n_pg), *([cv] * n_pg), *([clf] * n_pg))


def _s5_kernel(u_ref, lre_ref, lim_ref, ldt_ref, bre_ref, bim_ref, cre_ref, cim_ref, d_ref,
               x0r_ref, x0i_ref, y_ref, fr_ref, fi_ref,
               w_scr, c_scr, ar_scr, ai_scr, xs_scr, sr_scr, si_scr, *, nb, n_tc, sp):
    tc = pl.program_id(1)
    steps = xs_scr.shape[1] // 8
    nct = sp // LANES
    fold = 8 // nb
    ppp = nct // fold
    wf = ppp * LANES

    def seq_planes(b):
        for part in range(2):
            for j in range(nct):
                half, jj = divmod(j, ppp)
                yield (part * nct + j, part * ppp + jj, pl.ds(half * nb + b, steps, stride=8))

    def store_seq(b, val):
        for col, plane, rows_b in seq_planes(b):
            xs_scr[plane, rows_b, :] = val[:, col * LANES:(col + 1) * LANES]

    def load_seq(b):
        return jnp.concatenate([xs_scr[plane, rows_b, :] for _, plane, rows_b in seq_planes(b)], axis=1)

    def fold_tiles(x):
        if fold == 1:
            return x
        low = lax.broadcasted_iota(jnp.int32, (8, wf), 0) < nb
        return jnp.where(low, x[:, :wf], pltpu.roll(x[:, wf:], nb, 0))

    def unfold_tiles(x):
        if fold == 1:
            return x
        return jnp.concatenate([x, pltpu.roll(x, nb, 0)], axis=1)

    @pl.when(tc == 0)
    def _():
        lre = lre_ref[...]
        lim = lim_ref[...]
        dt = jnp.exp(ldt_ref[...])
        mag = jnp.exp(lre * dt)
        ar = mag * jnp.cos(lim * dt)
        ai = mag * jnp.sin(lim * dt)
        ar_scr[...] = ar
        ai_scr[...] = ai
        xr = ar - 1.0
        den = lre * lre + lim * lim
        cr = (xr * lre + ai * lim) / den
        ci = (ai * lre - xr * lim) / den
        uc, p_dim = bre_ref.shape
        c_dim = cre_ref.shape[1]

        def block_diag(x_ref, out_rows, out_cols, row_blk, col_blk):
            tr = lax.broadcasted_iota(jnp.int32, (col_blk, out_cols), 0)
            tcol = lax.broadcasted_iota(jnp.int32, (col_blk, out_cols), 1)
            rep = jnp.where((tcol & (col_blk - 1)) == tr, 1.0, 0.0).astype(BF16)
            hi, mid, lo = _split3(x_ref[...])
            full = _dot(hi, rep) + _dot(mid, rep) + _dot(lo, rep)
            rr = lax.broadcasted_iota(jnp.int32, (out_rows, out_cols), 0)
            cc = lax.broadcasted_iota(jnp.int32, (out_rows, out_cols), 1)
            same = (rr >> (row_blk.bit_length() - 1)) == (cc >> (col_blk.bit_length() - 1))
            return jnp.where(same, full, 0.0)

        bre = block_diag(bre_ref, uc, sp, c_dim, p_dim)
        bim = block_diag(bim_ref, uc, sp, c_dim, p_dim)
        w_scr[:, 0:sp] = (bre * cr - bim * ci).astype(BF16)
        w_scr[:, sp:2 * sp] = (bre * ci + bim * cr).astype(BF16)
        c_scr[0:sp, :] = block_diag(cre_ref, sp, uc, p_dim, c_dim).astype(BF16)
        c_scr[sp:2 * sp, :] = (-block_diag(cim_ref, sp, uc, p_dim, c_dim)).astype(BF16)
        sr_scr[...] = fold_tiles(x0r_ref[...])
        si_scr[...] = fold_tiles(x0i_ref[...])

    for b in range(nb):
        store_seq(b, _dot(u_ref[b].astype(BF16), w_scr[...]))
    arf = fold_tiles(jnp.broadcast_to(ar_scr[...], (8, sp)))
    aif = fold_tiles(jnp.broadcast_to(ai_scr[...], (8, sp)))

    def body(s, carry):
        pr, pi = carry
        r0 = pl.ds(pl.multiple_of(s * 8, 8), 8)
        br = jnp.concatenate([xs_scr[jj, r0, :] for jj in range(ppp)], axis=1)
        bi = jnp.concatenate([xs_scr[ppp + jj, r0, :] for jj in range(ppp)], axis=1)
        xr = arf * pr - aif * pi + br
        xi = arf * pi + aif * pr + bi
        for jj in range(ppp):
            xs_scr[jj, r0, :] = xr[:, jj * LANES:(jj + 1) * LANES]
            xs_scr[ppp + jj, r0, :] = xi[:, jj * LANES:(jj + 1) * LANES]
        return xr, xi

    fr, fi = lax.fori_loop(0, steps, body, (sr_scr[...], si_scr[...]), unroll=4)
    sr_scr[...] = fr
    si_scr[...] = fi
    for b in range(nb):
        y = _dot(load_seq(b).astype(BF16), c_scr[...]) + d_ref[...] * u_ref[b]
        y_ref[b] = jax.nn.gelu(y).astype(y_ref.dtype)

    @pl.when(tc == n_tc - 1)
    def _():
        fr_ref[...] = unfold_tiles(fr)
        fi_ref[...] = unfold_tiles(fi)


def _s5_layout(a_re, a_im, b_re, b_im, c_re, c_im, log_dt, gc):
    g, p = a_re.shape
    c = b_re.shape[-1]
    nch = g // gc
    assert p & (p - 1) == 0 and c & (c - 1) == 0, "block masks use shifts"
    row = lambda x: x.reshape(nch, 1, gc * p)
    b_rows = lambda b: b.transpose(0, 2, 1).reshape(nch, gc * c, p)
    c_rows = lambda cm: cm.transpose(0, 2, 1).reshape(nch, gc * p, c)
    return (row(a_re), row(a_im), row(jnp.repeat(log_dt, p)),
            b_rows(b_re), b_rows(b_im), c_rows(c_re), c_rows(c_im))


def _s5_scan(u, layout, d_skip, x0_re, x0_im, nb, steps):
    lre, lim, ldt, bre, bim, cre, cim = layout
    rows_total, d = u.shape
    nch, uc, p_dim = bre.shape
    sp, c_dim = cre.shape[1:]
    t_len = rows_total // nb
    n_tc = t_len // steps
    assert t_len % steps == 0 and (steps * nb) % 8 == 0 and nb in (4, 8)
    kern = functools.partial(_s5_kernel, nb=nb, n_tc=n_tc, sp=sp)
    par = lambda r, c: pl.BlockSpec((None, r, c), lambda g, t: (g, 0, 0))
    st_spec = pl.BlockSpec((8, sp), lambda g, t: (0, g))
    seq_spec = pl.BlockSpec((nb, steps, uc), lambda g, t: (0, t, g))
    y, f_re, f_im = pl.pallas_call(
        kern,
        grid=(nch, n_tc),
        in_specs=[seq_spec, par(1, sp), par(1, sp), par(1, sp),
                  par(uc, p_dim), par(uc, p_dim), par(sp, c_dim), par(sp, c_dim),
                  pl.BlockSpec((1, uc), lambda g, t: (0, g)),
                  st_spec, st_spec],
        out_specs=[seq_spec, st_spec, st_spec],
        out_shape=[jax.ShapeDtypeStruct((nb, t_len, d), BF16),
                   jax.ShapeDtypeStruct(x0_re.shape, F32), jax.ShapeDtypeStruct(x0_re.shape, F32)],
        scratch_shapes=[pltpu.VMEM((uc, 2 * sp), BF16), pltpu.VMEM((2 * sp, uc), BF16),
                        pltpu.VMEM((1, sp), F32), pltpu.VMEM((1, sp), F32),
                        pltpu.VMEM((2 * sp // LANES * nb // 8, steps * 8, LANES), F32),
                        pltpu.VMEM((8, sp * nb // 8), F32), pltpu.VMEM((8, sp * nb // 8), F32)],
        compiler_params=_params(56, ("parallel", "arbitrary")),
        name="s5_scan",
    )(u.reshape(nb, t_len, d), lre, lim, ldt, bre, bim, cre, cim, d_skip.reshape(1, d), x0_re, x0_im)
    return y.reshape(rows_total, d), f_re, f_im


def _row_tile(m, cap):
    best = None
    for t in range(16, cap + 1, 16):
        if m % t == 0:
            best = t
    return best if best is not None else m


def _col_tile(n, cap):
    best = LANES
    for t in range(LANES, cap + 1, LANES):
        if n % t == 0:
            best = t
    return best


def kernel(x_prompt, x_sample, cache_k, cache_v, cache_logf, state_s5_re, state_s5_im, page_table,
           meta_tokens, norm_g, ffn_w_up, ffn_w_down, attn_w_in, attn_b_f, attn_w_out,
           s5_A_re, s5_A_im, s5_B_re, s5_B_im, s5_C_re, s5_C_im, s5_log_dt, s5_D, s5_w_glu):
    bsz, seq, d = x_prompt.shape
    db, ds, _ = x_sample.shape
    n_meta = meta_tokens.shape[0]
    t_len = n_meta + seq
    mp, ms = bsz * t_len, db * ds
    depth = norm_g.shape[0]
    n_heads = attn_b_f.shape[-1]
    n_groups, state_dim = s5_A_re.shape[1:]

    tm = _row_tile(t_len, 704)
    tn_qkv = _col_tile(3 * d, 2048)
    tn_out = _col_tile(d, 1024)
    tn_glu = _col_tile(d, 512)
    tq = 256 if seq % 256 == 0 else LANES
    n_pages = page_table.shape[1]
    pages_per_step = max(p for p in (8, 4, 2, 1) if n_pages % p == 0)
    n_attn = (depth + 1) // 2

    w_in = attn_w_in.astype(BF16)
    w_out = attn_w_out.astype(BF16)
    w_glu = s5_w_glu.astype(BF16)

    meta = jnp.broadcast_to(meta_tokens[None].astype(x_prompt.dtype), (bsz, n_meta, d))
    hp = jnp.concatenate([meta, x_prompt], axis=1).reshape(mp, d)
    hs = x_sample.reshape(ms, d)

    lf_pr, sre_pr, sim_pr = [], [], []
    k_sa, v_sa, lf_sa, sre_sa, sim_sa = [], [], [], [], []
    kv_t = None
    up = _rmsnorm(hp, norm_g[0, 0], tm)
    us = _rmsnorm(hs, norm_g[0, 0], ms)
    for i in range(depth):
        g = norm_g[i]
        attn_layer = i % 2 == 0
        act_p, act_s, w_down = _ffn_up(up, us, ffn_w_up, ffn_w_down, i, 0, tm)
        hp, up, hs, us = _ffn_down(act_p, act_s, w_down, hp, hs, g[1], g[2],
                                   BF16 if attn_layer else F32, tm)
        if attn_layer:
            li = i // 2
            qkv_p, logf_p = _qkv_proj(up, w_in, attn_b_f, li, tm, tn_qkv)
            qkv_s, logf_s = _qkv_proj(us, w_in, attn_b_f, li, ms, tn_qkv)
            cparts = _prompt_cumsum(logf_p, bsz, t_len)
            yp, kv_t = _attn_prompt(qkv_p, cparts, bsz, t_len, n_meta, n_heads, tq, li, n_attn, kv_t)
            ys = _attn_decode(qkv_s, logf_s, cache_k, cache_v, cache_logf, page_table, li,
                              pages_per_step)
            hp, up, hs, us = _mixer_out(yp, ys, w_out, li, False, hp, hs, g[3], g[4], tm, tn_out)
            lf_pr.append(logf_p)
            k_sa.append(qkv_s[:, d:2 * d]); v_sa.append(qkv_s[:, 2 * d:]); lf_sa.append(logf_s)
        else:
            si = i // 2
            layout = _s5_layout(s5_A_re[si], s5_A_im[si], s5_B_re[si], s5_B_im[si],
                                s5_C_re[si], s5_C_im[si], s5_log_dt[si], 16)
            zeros = jnp.zeros((8, n_groups * state_dim), F32)
            yp, re_p, im_p = _s5_scan(up, layout, s5_D[si], zeros, zeros, bsz, tm)
            ys, re_s, im_s = _s5_scan(us, layout, s5_D[si],
                                      state_s5_re[si].reshape(db, -1), state_s5_im[si].reshape(db, -1),
                                      db, ds)
            hp, up, hs, us = _mixer_out(yp, ys, w_glu, si, True, hp, hs, g[3], g[4], tm, tn_glu)
            sre_pr.append(re_p[:bsz]); sim_pr.append(im_p[:bsz])
            sre_sa.append(re_s[:db]); sim_sa.append(im_s[:db])
        act_p, act_s, w_down = _ffn_up(up, us, ffn_w_up, ffn_w_down, i, 1, tm)
        g_next = norm_g[i + 1, 0] if i + 1 < depth else None
        hp, up, hs, us = _ffn_down(act_p, act_s, w_down, hp, hs, g[5], g_next, BF16, tm)

    hd = d // n_heads
    stack = lambda xs, shape: jnp.stack(xs).reshape((len(xs),) + shape)
    kv_out = lambda x: x.reshape(n_attn, bsz, n_heads, hd, t_len).transpose(0, 1, 4, 2, 3)
    return (hp.reshape(bsz, t_len, d)[:, n_meta:], hs.reshape(db, ds, d),
            kv_out(kv_t[0]), kv_out(kv_t[1]),
            stack(lf_pr, (bsz, t_len, n_heads)),
            stack(sre_pr, (bsz, n_groups, state_dim)), stack(sim_pr, (bsz, n_groups, state_dim)),
            stack(k_sa, (db, ds, n_heads, hd)), stack(v_sa, (db, ds, n_heads, hd)),
            stack(lf_sa, (db, ds, n_heads)),
            stack(sre_sa, (db, n_groups, state_dim)), stack(sim_sa, (db, n_groups, state_dim)))
```

```python
import functools

import jax
import jax.numpy as jnp
from jax import lax
from jax.experimental import pallas as pl
from jax.experimental.pallas import tpu as pltpu

F32 = jnp.float32
BF16 = jnp.bfloat16
RMS_EPS = 1e-6
LOG2E = 1.4426950408889634
LANES = 128
MIB = 1024 * 1024
NT_DIMS = (((1,), (1,)), ((), ()))


def _params(vmem_mib, semantics=None):
    return pltpu.CompilerParams(dimension_semantics=semantics, vmem_limit_bytes=vmem_mib * MIB)


def _rms(x, g):
    ms = jnp.mean(x * x, axis=-1, keepdims=True)
    return (x * lax.rsqrt(ms + RMS_EPS)) * g


def _split3(x):
    hi = x.astype(BF16)
    r1 = x - hi.astype(F32)
    mid = r1.astype(BF16)
    lo = (r1 - mid.astype(F32)).astype(BF16)
    return hi, mid, lo


def _dot(a, b):
    return jnp.dot(a, b, preferred_element_type=F32)


def _dot_nt(a, b):
    return lax.dot_general(a, b, NT_DIMS, preferred_element_type=F32)


def _rmsnorm_kernel(h_ref, g_ref, o_ref):
    o_ref[...] = _rms(h_ref[...], g_ref[...]).astype(o_ref.dtype)


def _rmsnorm(h, g, tm):
    m, d = h.shape
    return pl.pallas_call(
        _rmsnorm_kernel,
        grid=(m // tm,),
        in_specs=[pl.BlockSpec((tm, d), lambda i: (i, 0)), pl.BlockSpec((1, d), lambda i: (0, 0))],
        out_specs=pl.BlockSpec((tm, d), lambda i: (i, 0)),
        out_shape=jax.ShapeDtypeStruct((m, d), BF16),
        compiler_params=_params(40),
        name="rmsnorm",
    )(h, g.reshape(1, d))


def _ffn_up_kernel(up_ref, us_ref, wa_ref, wb_ref, wd_ref, op_ref, os_ref, wdb_ref, w_scr,
                   *, n_chunks, ch):
    w_scr[:, :LANES] = wa_ref[...].astype(BF16)
    w_scr[:, LANES:] = wb_ref[...].astype(BF16)
    wdb_ref[...] = wd_ref[...].astype(BF16)

    def swiglu(x):
        z = _dot(x, w_scr[...])
        a = z[:, :LANES]
        b = z[:, LANES:]
        return (a * jax.nn.sigmoid(a) * b).astype(BF16)

    def body(c, carry):
        rows = pl.ds(pl.multiple_of(c * ch, ch), ch)
        op_ref[rows, :] = swiglu(up_ref[rows, :])
        return carry

    lax.fori_loop(0, n_chunks, body, 0, unroll=True)
    os_ref[...] = swiglu(us_ref[...])


def _ffn_up(up, us, w_up, w_down, layer, half, ch):
    mp, d = up.shape
    ms = us.shape[0]
    f = w_up.shape[-1] // 2
    nt = f // LANES
    kern = functools.partial(_ffn_up_kernel, n_chunks=mp // ch, ch=ch)
    return pl.pallas_call(
        kern,
        grid=(nt,),
        in_specs=[
            pl.BlockSpec(memory_space=pltpu.VMEM),
            pl.BlockSpec(memory_space=pltpu.VMEM),
            pl.BlockSpec((None, None, d, LANES), lambda j: (layer, half, 0, j)),
            pl.BlockSpec((None, None, d, LANES), lambda j: (layer, half, 0, j + nt)),
            pl.BlockSpec((None, None, LANES, d), lambda j: (layer, half, j, 0)),
        ],
        out_specs=[pl.BlockSpec((mp, LANES), lambda j: (0, j)),
                   pl.BlockSpec((ms, LANES), lambda j: (0, j)),
                   pl.BlockSpec((LANES, d), lambda j: (j, 0))],
        out_shape=[jax.ShapeDtypeStruct((mp, f), BF16), jax.ShapeDtypeStruct((ms, f), BF16),
                   jax.ShapeDtypeStruct((f, d), BF16)],
        scratch_shapes=[pltpu.VMEM((d, 2 * LANES), BF16)],
        compiler_params=_params(58),
        name="ffn_up",
    )(up, us, w_up, w_up, w_down)


def _residual_epilogue(y, h_ref, gp_ref, gn_ref, hn_ref, un_ref, scale):
    hn = h_ref[...] + scale * _rms(y, gp_ref[...])
    hn_ref[...] = hn
    if un_ref is not None:
        un_ref[...] = _rms(hn, gn_ref[...]).astype(un_ref.dtype)


def _down_tiling(f):
    nt = f // LANES
    for dmul in (8, 6, 4, 2, 7, 5, 3):
        if nt % dmul == 0 and nt // dmul >= 1:
            return dmul * LANES, nt // dmul, False
    for dmul in (8, 6, 4, 2, 7, 5, 3):
        if (nt - 1) % dmul == 0:
            return dmul * LANES, (nt - 1) // dmul, True
    return LANES, nt, False


class _RowRefs:
    def __init__(self, x, xt, h, hn, un, acc):
        self.x, self.xt, self.h, self.hn, self.un, self.acc = x, xt, h, hn, un, acc


def _unpack_rows(refs, has_tail, has_next):
    refs = list(refs)
    take = lambda cond=True: refs.pop(0) if cond else None
    xp, xtp, xs, xts = take(), take(has_tail), take(), take(has_tail)
    return refs, xp, xtp, xs, xts


def _ffn_down_kernel(*refs, nk, scale, has_tail, has_next):
    refs, xp, xtp, xs, xts = _unpack_rows(refs, has_tail, has_next)
    take = lambda cond=True: refs.pop(0) if cond else None
    wm_ref, wt_ref = take(), take(has_tail)
    hp, hs, gp_ref, gn_ref = take(), take(), take(), take(has_next)
    hnp, unp, hns, uns = take(), take(has_next), take(), take(has_next)
    accp, accs = take(), take()
    prompt = _RowRefs(xp, xtp, hp, hnp, unp, accp)
    sample = _RowRefs(xs, xts, hs, hns, uns, accs)
    i = pl.program_id(0)
    k = pl.program_id(1)

    def accumulate(r):
        @pl.when(k == 0)
        def _():
            if has_tail:
                r.acc[...] = _dot(r.xt[...], wt_ref[...])
            else:
                r.acc[...] = jnp.zeros(r.acc.shape, F32)

        r.acc[...] += _dot(r.x[...], wm_ref[...])

        @pl.when(k == nk - 1)
        def _():
            _residual_epilogue(r.acc[...], r.h, gp_ref, gn_ref, r.hn, r.un, scale)

    accumulate(prompt)
    pl.when(i == 0)(lambda: accumulate(sample))


def _ffn_down(act_p, act_s, w_down, h_p, h_s, g_post, g_next, u_dtype, tm):
    mp, f = act_p.shape
    ms = act_s.shape[0]
    d = h_p.shape[1]
    tk, nk, has_tail = _down_tiling(f)
    has_next = g_next is not None
    tail_blk = (f - LANES) // LANES
    in_specs = [pl.BlockSpec((tm, tk), lambda i, k: (i, k))]
    args = [act_p]
    if has_tail:
        in_specs.append(pl.BlockSpec((tm, LANES), lambda i, k: (i, tail_blk)))
        args.append(act_p)
    in_specs.append(pl.BlockSpec((ms, tk), lambda i, k: (0, k)))
    args.append(act_s)
    if has_tail:
        in_specs.append(pl.BlockSpec((ms, LANES), lambda i, k: (0, tail_blk)))
        args.append(act_s)
    in_specs.append(pl.BlockSpec((tk, d), lambda i, k: (k, 0)))
    args.append(w_down)
    if has_tail:
        in_specs.append(pl.BlockSpec((LANES, d), lambda i, k: (tail_blk, 0)))
        args.append(w_down)
    row_spec = pl.BlockSpec((tm, d), lambda i, k: (i, 0))
    srow_spec = pl.BlockSpec((ms, d), lambda i, k: (0, 0))
    vec_spec = pl.BlockSpec((1, d), lambda i, k: (0, 0))
    in_specs += [row_spec, srow_spec, vec_spec]
    args += [h_p, h_s, g_post.reshape(1, d)]
    out_specs = [row_spec]
    out_shape = [jax.ShapeDtypeStruct((mp, d), F32)]
    if has_next:
        in_specs.append(vec_spec)
        args.append(g_next.reshape(1, d))
        out_specs.append(row_spec)
        out_shape.append(jax.ShapeDtypeStruct((mp, d), u_dtype))
    out_specs.append(srow_spec)
    out_shape.append(jax.ShapeDtypeStruct((ms, d), F32))
    if has_next:
        out_specs.append(srow_spec)
        out_shape.append(jax.ShapeDtypeStruct((ms, d), u_dtype))
    kern = functools.partial(_ffn_down_kernel, nk=nk, scale=0.5, has_tail=has_tail, has_next=has_next)
    res = pl.pallas_call(
        kern,
        grid=(mp // tm, nk),
        in_specs=in_specs,
        out_specs=out_specs,
        out_shape=out_shape,
        scratch_shapes=[pltpu.VMEM((tm, d), F32), pltpu.VMEM((ms, d), F32)],
        compiler_params=_params(56, ("arbitrary", "arbitrary")),
        name="ffn_down",
    )(*args)
    if has_next:
        return res[0], res[1], res[2], res[3]
    return res[0], None, res[1], None


def _mixer_out_kernel(xp, xs, *refs, nn, tn, glu):
    refs = list(refs)
    take = lambda cond=True: refs.pop(0) if cond else None
    wa_ref, wb_ref = take(), take(glu)
    hp, hs, gp_ref, gn_ref = take(), take(), take(), take()
    hnp, unp, hns, uns = take(), take(), take(), take()
    accp, accs = take(), take()
    prompt = _RowRefs(xp, None, hp, hnp, unp, accp)
    sample = _RowRefs(xs, None, hs, hns, uns, accs)
    i = pl.program_id(0)
    j = pl.program_id(1)
    col = pl.multiple_of(j * tn, tn)

    def project(r):
        x = r.x[...].astype(BF16)
        if glu:
            y = _dot(x, wa_ref[...]) * jax.nn.sigmoid(_dot(x, wb_ref[...]))
        else:
            y = _dot(x, wa_ref[...])
        r.acc[:, pl.ds(col, tn)] = y

        @pl.when(j == nn - 1)
        def _():
            _residual_epilogue(r.acc[...], r.h, gp_ref, gn_ref, r.hn, r.un, 1.0)

    project(prompt)
    pl.when(i == 0)(lambda: project(sample))


def _mixer_out(x_p, x_s, w, layer, glu, h_p, h_s, g_post, g_next, tm, tn):
    mp, d = h_p.shape
    ms = h_s.shape[0]
    kdim = x_s.shape[1]
    nn = d // tn
    row_spec = pl.BlockSpec((tm, d), lambda i, j: (i, 0))
    srow_spec = pl.BlockSpec((ms, d), lambda i, j: (0, 0))
    vec_spec = pl.BlockSpec((1, d), lambda i, j: (0, 0))
    in_specs = [pl.BlockSpec((tm, kdim), lambda i, j: (i, 0)),
                pl.BlockSpec((ms, kdim), lambda i, j: (0, 0)),
                pl.BlockSpec((None, kdim, tn), lambda i, j: (layer, 0, j))]
    args = [x_p, x_s, w]
    if glu:
        in_specs.append(pl.BlockSpec((None, kdim, tn), lambda i, j: (layer, 0, j + nn)))
        args.append(w)
    in_specs += [row_spec, srow_spec, vec_spec, vec_spec]
    args += [h_p, h_s, g_post.reshape(1, d), g_next.reshape(1, d)]
    return pl.pallas_call(
        functools.partial(_mixer_out_kernel, nn=nn, tn=tn, glu=glu),
        grid=(mp // tm, nn),
        in_specs=in_specs,
        out_specs=[row_spec, row_spec, srow_spec, srow_spec],
        out_shape=[jax.ShapeDtypeStruct((mp, d), F32), jax.ShapeDtypeStruct((mp, d), BF16),
                   jax.ShapeDtypeStruct((ms, d), F32), jax.ShapeDtypeStruct((ms, d), BF16)],
        scratch_shapes=[pltpu.VMEM((tm, d), F32), pltpu.VMEM((ms, d), F32)],
        compiler_params=_params(56, ("arbitrary", "arbitrary")),
        name="mixer_out_glu" if glu else "mixer_out",
    )(*args)


def _log_sigmoid(x):
    return jnp.minimum(x, 0.0) - jnp.log1p(jnp.exp(-jnp.abs(x)))


def _qkv_kernel(x_ref, w_ref, wf_ref, bf_ref, o_ref, lf_ref, *, n_heads):
    j = pl.program_id(1)
    x = x_ref[...]
    o_ref[...] = _dot(x, w_ref[...])

    @pl.when(j == 0)
    def _():
        f = _dot(x, wf_ref[...])[:, :n_heads] + bf_ref[...]
        lf_ref[...] = _log_sigmoid(f)


def _qkv_proj(u, w_in, b_f, layer, tm, tn):
    m, d = u.shape
    n_heads = b_f.shape[-1]
    n_main = 3 * d
    return pl.pallas_call(
        functools.partial(_qkv_kernel, n_heads=n_heads),
        grid=(m // tm, n_main // tn),
        in_specs=[
            pl.BlockSpec((tm, d), lambda i, j: (i, 0)),
            pl.BlockSpec((None, d, tn), lambda i, j: (layer, 0, j)),
            pl.BlockSpec((None, d, LANES), lambda i, j: (layer, 0, n_main // LANES)),
            pl.BlockSpec((None, 1, n_heads), lambda i, j: (layer, 0, 0)),
        ],
        out_specs=[pl.BlockSpec((tm, tn), lambda i, j: (i, j)),
                   pl.BlockSpec((tm, n_heads), lambda i, j: (i, 0))],
        out_shape=[jax.ShapeDtypeStruct((m, n_main), F32), jax.ShapeDtypeStruct((m, n_heads), F32)],
        compiler_params=_params(48, ("parallel", "arbitrary")),
        name="qkv_proj",
    )(u, w_in, w_in, b_f.reshape(b_f.shape[0], 1, n_heads))


def _cumsum_kernel(lf_ref, o_ref, *, t_len, chunk, n_heads):
    r = lax.broadcasted_iota(jnp.int32, (chunk, chunk), 0)
    c = lax.broadcasted_iota(jnp.int32, (chunk, chunk), 1)
    tri = jnp.where(c <= r, 1.0, 0.0).astype(BF16)
    carry = jnp.zeros((1, n_heads), F32)
    for s in range(t_len // chunk):
        x = lf_ref[s * chunk:(s + 1) * chunk, :]
        hi, mid, lo = _split3(x)
        cs = _dot(tri, hi) + _dot(tri, mid) + _dot(tri, lo) + carry
        carry = cs[chunk - 1:chunk, :]
        chi, cmid, clo = _split3(cs * LOG2E)
        o_ref[s * chunk:(s + 1) * chunk, 0:n_heads] = chi
        o_ref[s * chunk:(s + 1) * chunk, n_heads:2 * n_heads] = cmid
        o_ref[s * chunk:(s + 1) * chunk, 2 * n_heads:3 * n_heads] = clo


def _cumsum_chunk(t_len):
    for c in (688, 256, 128, 64, 48, 16):
        if t_len % c == 0:
            return c
    return t_len


def _prompt_cumsum(logf, batch, t_len):
    n_heads = logf.shape[1]
    kern = functools.partial(_cumsum_kernel, t_len=t_len, chunk=_cumsum_chunk(t_len), n_heads=n_heads)
    return pl.pallas_call(
        kern,
        grid=(batch,),
        in_specs=[pl.BlockSpec((t_len, n_heads), lambda b: (b, 0))],
        out_specs=pl.BlockSpec((t_len, 3 * n_heads), lambda b: (b, 0)),
        out_shape=jax.ShapeDtypeStruct((batch * t_len, 3 * n_heads), BF16),
        compiler_params=_params(32, ("parallel",)),
        name="prompt_cumsum",
    )(logf)


def _store_transposed(src_ref, dst_ref, t_len):
    n_full = t_len // LANES
    for c in range(n_full):
        dst_ref[:, c * LANES:(c + 1) * LANES] = src_ref[c * LANES:(c + 1) * LANES, :].T
    rem = t_len - n_full * LANES
    if rem:
        last = src_ref[t_len - LANES:t_len, :].T
        dst_ref[:, n_full * LANES:t_len] = last[:, LANES - rem:]


def _attn_prompt_kernel(q_ref, k_ref, v_ref, cp_ref, *rest, t_len, tq, n_heads, head_dim, aliased):
    if aliased:
        rest = rest[2:]
    o_ref, kt_ref, vt_ref, qa_scr, ka_scr, v_scr = rest
    _store_transposed(k_ref, kt_ref, t_len)
    _store_transposed(v_ref, vt_ref, t_len)
    hp = pl.program_id(1)
    half = LANES // 2
    lane = lax.broadcasted_iota(jnp.int32, (1, LANES), 1)
    r3 = lax.broadcasted_iota(jnp.int32, (3 * n_heads, LANES), 0)
    l3 = lax.broadcasted_iota(jnp.int32, (3 * n_heads, LANES), 1)
    cp = cp_ref[...]
    q = (q_ref[...] * (head_dim ** -0.5 * LOG2E)).astype(BF16)
    k = k_ref[...].astype(BF16)
    v_scr[...] = v_ref[...].astype(BF16)
    for hh in range(2):
        head = 2 * hp + hh
        base = half * (1 - hh)
        sel_q = jnp.zeros((3 * n_heads, LANES), F32)
        sel_k = jnp.zeros((3 * n_heads, LANES), F32)
        for part in range(3):
            row_hit = r3 == head + part * n_heads
            sel_q = sel_q + jnp.where(row_hit & (l3 == base + part), 1.0, 0.0)
            sel_k = sel_k + jnp.where(row_hit & (l3 == base + 3 + part), -1.0, 0.0)
        ones_q = jnp.where((lane >= base + 3) & (lane < base + 6), 1.0, 0.0)
        ones_k = jnp.where((lane >= base) & (lane < base + 3), 1.0, 0.0)
        ex_q = (_dot(cp, sel_q.astype(BF16)) + ones_q).astype(BF16)
        ex_k = (_dot(cp, sel_k.astype(BF16)) + ones_k).astype(BF16)
        data = (lane >= half * hh) & (lane < half * hh + half)
        qa_scr[hh] = jnp.where(data, q, ex_q)
        ka_scr[hh] = jnp.where(data, k, ex_k)

    n_tiles = t_len // tq
    tiles = [(i * tq, tq) for i in range(n_tiles - 1)] + [((n_tiles - 1) * tq, tq + t_len % tq)]
    units = [(q0, tl, hh) for q0, tl in tiles for hh in range(2)]

    def scores(q0, tl, hh):
        qt = qa_scr[hh, q0:q0 + tl, :]
        row = lax.broadcasted_iota(jnp.int32, (tl, tl), 0)
        col = lax.broadcasted_iota(jnp.int32, (tl, tl), 1)
        sd = jnp.where(col <= row, _dot_nt(qt, ka_scr[hh, q0:q0 + tl, :]), -jnp.inf)
        so = _dot_nt(qt, ka_scr[hh, 0:q0, :]) if q0 > 0 else None
        return sd, so

    def softmax(sd, so):
        m = jnp.max(sd, axis=-1, keepdims=True)
        if so is not None:
            m = jnp.maximum(m, jnp.max(so, axis=-1, keepdims=True))
        pd = jnp.exp2(sd - m)
        den = jnp.sum(pd, axis=-1, keepdims=True)
        po = None
        if so is not None:
            po = jnp.exp2(so - m)
            den = den + jnp.sum(po, axis=-1, keepdims=True)
            po = po.astype(BF16)
        return pd.astype(BF16), po, den

    def values(q0, tl, pd, po, den):
        o = _dot(pd, v_scr[q0:q0 + tl, :])
        if po is not None:
            o = o + _dot(po, v_scr[0:q0, :])
        return o / den

    staged_s, staged_p, done = {}, {}, {}
    for k in range(len(units) + 2):
        if k < len(units):
            staged_s[k] = scores(*units[k])
        if 0 <= k - 1 < len(units):
            staged_p[k - 1] = softmax(*staged_s.pop(k - 1))
        if 0 <= k - 2 < len(units):
            q0, tl, hh = units[k - 2]
            done[hh] = values(q0, tl, *staged_p.pop(k - 2))
            if hh == 1:
                o_ref[q0:q0 + tl, :] = jnp.where(lane < half, done[0], done[1]).astype(o_ref.dtype)


def _attn_prompt(qkv, cparts, batch, t_len, n_heads, tq, layer, n_layers, kv_t):
    d = qkv.shape[1] // 3
    head_dim = d // n_heads
    assert 2 * head_dim == LANES, "head pairs must fill one lane tile"
    assert t_len % 16 == 0 and t_len >= LANES
    npair = n_heads // 2
    aliased = kv_t is not None
    kern = functools.partial(_attn_prompt_kernel, t_len=t_len, tq=tq,
                             n_heads=n_heads, head_dim=head_dim, aliased=aliased)
    blk = lambda off: pl.BlockSpec((t_len, LANES), lambda b, p: (b, off + p))
    in_specs = [blk(0), blk(npair), blk(2 * npair),
                pl.BlockSpec((t_len, 3 * n_heads), lambda b, p: (b, 0))]
    args = [qkv, qkv, qkv, cparts]
    aliases = {}
    if aliased:
        in_specs += [pl.BlockSpec(memory_space=pl.ANY), pl.BlockSpec(memory_space=pl.ANY)]
        args += list(kv_t)
        aliases = {4: 1, 5: 2}
    t_spec = pl.BlockSpec((None, None, LANES, t_len), lambda b, p: (layer, b, p, 0))
    t_shape = jax.ShapeDtypeStruct((n_layers, batch, d, t_len), F32)
    y, k_t, v_t = pl.pallas_call(
        kern,
        grid=(batch, npair),
        in_specs=in_specs,
        out_specs=[pl.BlockSpec((t_len, LANES), lambda b, p: (b, p)), t_spec, t_spec],
        out_shape=[jax.ShapeDtypeStruct((batch * t_len, d), BF16), t_shape, t_shape],
        scratch_shapes=[pltpu.VMEM((2, t_len, LANES), BF16), pltpu.VMEM((2, t_len, LANES), BF16),
                        pltpu.VMEM((t_len, LANES), BF16)],
        input_output_aliases=aliases,
        compiler_params=_params(48, ("parallel", "arbitrary")),
        name="attn_prompt",
    )(*args)
    return y, (k_t, v_t)


def _attn_decode_kernel(pt_ref, q_ref, kn_ref, vn_ref, lfn_ref, *rest,
                        n_pg, n_heads, head_dim, n_new, page, n_steps):
    del pt_ref
    k_refs = rest[:n_pg]
    v_refs = rest[n_pg:2 * n_pg]
    lf_refs = rest[2 * n_pg:3 * n_pg]
    o_ref = rest[3 * n_pg]
    (q_scr, s_scr, p_scr, acc_scr, m_scr, l_scr, cnew_scr, alpha_scr,
     tail_scr, b_scr) = rest[3 * n_pg + 1:]
    step = pl.program_id(1)
    rows = n_heads * n_new
    head_unroll = 8 if n_heads % 8 == 0 else 1
    er = lax.broadcasted_iota(jnp.int32, (rows, n_heads), 0)
    ec = lax.broadcasted_iota(jnp.int32, (rows, n_heads), 1)
    expand = jnp.where((er >= ec * n_new) & (er < (ec + 1) * n_new), 1.0, 0.0).astype(BF16)

    def spread(lf):
        return [_dot_nt(expand, part).astype(BF16) for part in _split3(lf)]

    def head_update(h8, p_h, v_h, first):
        contrib = _dot(p_h, v_h)
        if first:
            acc_scr[h8, :] = contrib
        else:
            acc_scr[h8, :] = alpha_scr[h8, 0:head_dim] * acc_scr[h8, :] + contrib

    @pl.when(step == 0)
    def _():
        scale = head_dim ** -0.5
        for h in range(n_heads):
            q_scr[h * n_new:(h + 1) * n_new, :] = q_ref[:, h * head_dim:(h + 1) * head_dim] * scale
        kr = lax.broadcasted_iota(jnp.int32, (n_new, LANES), 0)
        kc = lax.broadcasted_iota(jnp.int32, (n_new, LANES), 1)
        upper = jnp.where(kr <= kc, 1.0, 0.0).astype(BF16)
        parts = spread(lfn_ref[...])
        cum = _dot(parts[0], upper) + _dot(parts[1], upper) + _dot(parts[2], upper)
        rq = lax.broadcasted_iota(jnp.int32, (rows, LANES), 0)
        lq = lax.broadcasted_iota(jnp.int32, (rows, LANES), 1)
        qpos = rq & (n_new - 1)
        c_q = jnp.sum(jnp.where(lq == qpos, cum, 0.0), axis=-1, keepdims=True)
        cnew_scr[...] = jnp.broadcast_to(c_q, (rows, LANES))
        bias = c_q - cum
        valid = lq <= qpos
        s_scr[:, 0:LANES] = jnp.zeros((rows, LANES), F32)
        for h in range(n_heads):
            h8 = pl.ds(h * n_new, n_new)
            k_h = kn_ref[:, h * head_dim:(h + 1) * head_dim].astype(BF16)
            s_scr[h8, 0:n_new] = _dot_nt(q_scr[h8, :].astype(BF16), k_h)
        s = jnp.where(valid, s_scr[:, 0:LANES] + bias, -jnp.inf)
        m = jnp.max(s, axis=-1, keepdims=True)
        p = jnp.exp(s - m)
        m_scr[...] = jnp.broadcast_to(m, (rows, LANES))
        l_scr[...] = p
        tail_scr[...] = jnp.zeros((n_heads, LANES), F32)
        p_scr[:, 0:LANES] = p
        for h in range(n_heads):
            h8 = pl.ds(h * n_new, n_new)
            v_h = vn_ref[:, h * head_dim:(h + 1) * head_dim].astype(BF16)
            head_update(h8, p_scr[h8, 0:n_new].astype(BF16), v_h, True)

    kr = lax.broadcasted_iota(jnp.int32, (page, 2 * page), 0)
    kc = lax.broadcasted_iota(jnp.int32, (page, 2 * page), 1)
    after = jnp.where((kr > kc) | (kc >= page), 1.0, 0.0).astype(BF16)

    tail = tail_scr[...]
    for j in range(n_pg):
        hi, mid, lo = _split3(lf_refs[j][...])
        full = _dot(hi, after) + _dot(mid, after) + _dot(lo, after)
        b_scr[:, j * page:(j + 1) * page] = full[:, 0:page] + tail
        tail = tail + full[:, page:2 * page]
    tail_scr[...] = tail

    def scores(h, carry):
        h8 = pl.ds(pl.multiple_of(h * n_new, n_new), n_new)
        q_h = q_scr[h8, :].astype(BF16)
        k_h = jnp.concatenate([k_refs[j][h] for j in range(n_pg)], axis=1).astype(BF16)
        s_scr[h8, :] = _dot(q_h, k_h) + b_scr[pl.ds(h, 1), :]
        return carry

    lax.fori_loop(0, n_heads, scores, 0, unroll=head_unroll)

    cnew = cnew_scr[...]
    s_all = [s_scr[:, j * page:(j + 1) * page] + cnew for j in range(n_pg)]
    m_chunk = s_all[0]
    for j in range(1, n_pg):
        m_chunk = jnp.maximum(m_chunk, s_all[j])
    m_old = m_scr[...]
    m_new = jnp.maximum(m_old, jnp.max(m_chunk, axis=-1, keepdims=True))
    alpha = jnp.exp(m_old - m_new)
    m_scr[...] = m_new
    alpha_scr[...] = alpha
    l_new = alpha * l_scr[...]
    for j in range(n_pg):
        p = jnp.exp(s_all[j] - m_new)
        l_new = l_new + p
        p_scr[:, j * page:(j + 1) * page] = p
    l_scr[...] = l_new

    def values(h, carry):
        h8 = pl.ds(pl.multiple_of(h * n_new, n_new), n_new)
        v_h = jnp.concatenate([v_refs[j][h] for j in range(n_pg)], axis=1).astype(BF16)
        contrib = _dot_nt(p_scr[h8, :].astype(BF16), v_h)
        acc_scr[h8, :] = alpha_scr[h8, 0:head_dim] * acc_scr[h8, :] + contrib
        return carry

    lax.fori_loop(0, n_heads, values, 0, unroll=head_unroll)

    @pl.when(step == n_steps - 1)
    def _():
        den = jnp.sum(l_scr[...], axis=-1, keepdims=True)
        res = acc_scr[...] / den
        for h in range(n_heads):
            o_ref[:, h * head_dim:(h + 1) * head_dim] = res[h * n_new:(h + 1) * n_new, :]


def _attn_decode(qkv_s, logf_s, cache_k, cache_v, cache_logf, page_table, layer, n_pg):
    n_layers, n_pool, page, n_heads, head_dim = cache_k.shape
    db, n_pages = page_table.shape
    n_new = qkv_s.shape[0] // db
    d = n_heads * head_dim
    rows = n_heads * n_new
    n_steps = n_pages // n_pg
    ck = cache_k.transpose(0, 1, 3, 4, 2)
    cv = cache_v.transpose(0, 1, 3, 4, 2)
    clf = cache_logf.transpose(0, 1, 3, 2)

    def page_of(b, s, pt, j):
        return pt[b * n_pages + (n_pages - 1 - (s * n_pg + j))]

    kv_spec = lambda j: pl.BlockSpec((None, None, n_heads, head_dim, page),
                                     lambda b, s, pt: (layer, page_of(b, s, pt, j), 0, 0, 0))
    lf_spec = lambda j: pl.BlockSpec((None, None, n_heads, page),
                                     lambda b, s, pt: (layer, page_of(b, s, pt, j), 0, 0))
    new_spec = lambda c: pl.BlockSpec((n_new, d), lambda b, s, pt: (b, c))
    in_specs = [new_spec(0), new_spec(1), new_spec(2),
                pl.BlockSpec((n_new, n_heads), lambda b, s, pt: (b, 0))]
    in_specs += [kv_spec(j) for j in range(n_pg)] + [kv_spec(j) for j in range(n_pg)]
    in_specs += [lf_spec(j) for j in range(n_pg)]
    kern = functools.partial(_attn_decode_kernel, n_pg=n_pg, n_heads=n_heads, head_dim=head_dim,
                             n_new=n_new, page=page, n_steps=n_steps)
    stat = pltpu.VMEM((rows, LANES), F32)
    grid_spec = pltpu.PrefetchScalarGridSpec(
        num_scalar_prefetch=1,
        grid=(db, n_steps),
        in_specs=in_specs,
        out_specs=pl.BlockSpec((n_new, d), lambda b, s, pt: (b, 0)),
        scratch_shapes=[pltpu.VMEM((rows, head_dim), F32),
                        pltpu.VMEM((rows, n_pg * page), F32),
                        pltpu.VMEM((rows, n_pg * page), F32),
                        pltpu.VMEM((rows, head_dim), F32),
                        stat, stat, stat, stat,
                        pltpu.VMEM((n_heads, LANES), F32),
                        pltpu.VMEM((n_heads, n_pg * page), F32)],
    )
    return pl.pallas_call(
        kern,
        grid_spec=grid_spec,
        out_shape=jax.ShapeDtypeStruct((db * n_new, d), F32),
        compiler_params=_params(56, ("parallel", "arbitrary")),
        name="attn_decode",
    )(page_table.reshape(-1), qkv_s, qkv_s, qkv_s, logf_s,
      *([ck] * n_pg), *([cv] * n_pg), *([clf] * n_pg))


def _s5_kernel(u_ref, lre_ref, lim_ref, ldt_ref, bre_ref, bim_ref, cre_ref, cim_ref, d_ref,
               x0r_ref, x0i_ref, y_ref, fr_ref, fi_ref,
               w_scr, c_scr, ar_scr, ai_scr, xs_scr, sr_scr, si_scr, *, nb, n_tc, sp):
    tc = pl.program_id(1)
    steps = xs_scr.shape[1] // 8
    nct = sp // LANES
    fold = 8 // nb
    ppp = nct // fold
    wf = ppp * LANES

    def seq_planes(b):
        for part in range(2):
            for j in range(nct):
                half, jj = divmod(j, ppp)
                yield (part * nct + j, part * ppp + jj, pl.ds(half * nb + b, steps, stride=8))

    def store_seq(b, val):
        for col, plane, rows_b in seq_planes(b):
            xs_scr[plane, rows_b, :] = val[:, col * LANES:(col + 1) * LANES]

    def load_seq(b):
        return jnp.concatenate([xs_scr[plane, rows_b, :] for _, plane, rows_b in seq_planes(b)], axis=1)

    def fold_tiles(x):
        if fold == 1:
            return x
        low = lax.broadcasted_iota(jnp.int32, (8, wf), 0) < nb
        return jnp.where(low, x[:, :wf], pltpu.roll(x[:, wf:], nb, 0))

    def unfold_tiles(x):
        if fold == 1:
            return x
        return jnp.concatenate([x, pltpu.roll(x, nb, 0)], axis=1)

    @pl.when(tc == 0)
    def _():
        lre = lre_ref[...]
        lim = lim_ref[...]
        dt = jnp.exp(ldt_ref[...])
        mag = jnp.exp(lre * dt)
        ar = mag * jnp.cos(lim * dt)
        ai = mag * jnp.sin(lim * dt)
        ar_scr[...] = ar
        ai_scr[...] = ai
        xr = ar - 1.0
        den = lre * lre + lim * lim
        cr = (xr * lre + ai * lim) / den
        ci = (ai * lre - xr * lim) / den
        uc, p_dim = bre_ref.shape
        c_dim = cre_ref.shape[1]

        def block_diag(x_ref, out_rows, out_cols, row_blk, col_blk):
            tr = lax.broadcasted_iota(jnp.int32, (col_blk, out_cols), 0)
            tcol = lax.broadcasted_iota(jnp.int32, (col_blk, out_cols), 1)
            rep = jnp.where((tcol & (col_blk - 1)) == tr, 1.0, 0.0).astype(BF16)
            hi, mid, lo = _split3(x_ref[...])
            full = _dot(hi, rep) + _dot(mid, rep) + _dot(lo, rep)
            rr = lax.broadcasted_iota(jnp.int32, (out_rows, out_cols), 0)
            cc = lax.broadcasted_iota(jnp.int32, (out_rows, out_cols), 1)
            same = (rr >> (row_blk.bit_length() - 1)) == (cc >> (col_blk.bit_length() - 1))
            return jnp.where(same, full, 0.0)

        bre = block_diag(bre_ref, uc, sp, c_dim, p_dim)
        bim = block_diag(bim_ref, uc, sp, c_dim, p_dim)
        w_scr[:, 0:sp] = (bre * cr - bim * ci).astype(BF16)
        w_scr[:, sp:2 * sp] = (bre * ci + bim * cr).astype(BF16)
        c_scr[0:sp, :] = block_diag(cre_ref, sp, uc, p_dim, c_dim).astype(BF16)
        c_scr[sp:2 * sp, :] = (-block_diag(cim_ref, sp, uc, p_dim, c_dim)).astype(BF16)
        sr_scr[...] = fold_tiles(x0r_ref[...])
        si_scr[...] = fold_tiles(x0i_ref[...])

    for b in range(nb):
        store_seq(b, _dot(u_ref[b].astype(BF16), w_scr[...]))
    arf = fold_tiles(jnp.broadcast_to(ar_scr[...], (8, sp)))
    aif = fold_tiles(jnp.broadcast_to(ai_scr[...], (8, sp)))

    def body(s, carry):
        pr, pi = carry
        r0 = pl.ds(pl.multiple_of(s * 8, 8), 8)
        br = jnp.concatenate([xs_scr[jj, r0, :] for jj in range(ppp)], axis=1)
        bi = jnp.concatenate([xs_scr[ppp + jj, r0, :] for jj in range(ppp)], axis=1)
        xr = arf * pr - aif * pi + br
        xi = arf * pi + aif * pr + bi
        for jj in range(ppp):
            xs_scr[jj, r0, :] = xr[:, jj * LANES:(jj + 1) * LANES]
            xs_scr[ppp + jj, r0, :] = xi[:, jj * LANES:(jj + 1) * LANES]
        return xr, xi

    fr, fi = lax.fori_loop(0, steps, body, (sr_scr[...], si_scr[...]), unroll=4)
    sr_scr[...] = fr
    si_scr[...] = fi
    for b in range(nb):
        y = _dot(load_seq(b).astype(BF16), c_scr[...]) + d_ref[...] * u_ref[b]
        y_ref[b] = jax.nn.gelu(y).astype(y_ref.dtype)

    @pl.when(tc == n_tc - 1)
    def _():
        fr_ref[...] = unfold_tiles(fr)
        fi_ref[...] = unfold_tiles(fi)


def _s5_layout(a_re, a_im, b_re, b_im, c_re, c_im, log_dt, gc):
    g, p = a_re.shape
    c = b_re.shape[-1]
    nch = g // gc
    assert p & (p - 1) == 0 and c & (c - 1) == 0, "block masks use shifts"
    row = lambda x: x.reshape(nch, 1, gc * p)
    b_rows = lambda b: b.transpose(0, 2, 1).reshape(nch, gc * c, p)
    c_rows = lambda cm: cm.transpose(0, 2, 1).reshape(nch, gc * p, c)
    return (row(a_re), row(a_im), row(jnp.repeat(log_dt, p)),
            b_rows(b_re), b_rows(b_im), c_rows(c_re), c_rows(c_im))


def _s5_scan(u, layout, d_skip, x0_re, x0_im, nb, steps):
    lre, lim, ldt, bre, bim, cre, cim = layout
    rows_total, d = u.shape
    nch, uc, p_dim = bre.shape
    sp, c_dim = cre.shape[1:]
    t_len = rows_total // nb
    n_tc = t_len // steps
    assert t_len % steps == 0 and (steps * nb) % 8 == 0 and nb in (4, 8)
    kern = functools.partial(_s5_kernel, nb=nb, n_tc=n_tc, sp=sp)
    par = lambda r, c: pl.BlockSpec((None, r, c), lambda g, t: (g, 0, 0))
    st_spec = pl.BlockSpec((8, sp), lambda g, t: (0, g))
    seq_spec = pl.BlockSpec((nb, steps, uc), lambda g, t: (0, t, g))
    y, f_re, f_im = pl.pallas_call(
        kern,
        grid=(nch, n_tc),
        in_specs=[seq_spec, par(1, sp), par(1, sp), par(1, sp),
                  par(uc, p_dim), par(uc, p_dim), par(sp, c_dim), par(sp, c_dim),
                  pl.BlockSpec((1, uc), lambda g, t: (0, g)),
                  st_spec, st_spec],
        out_specs=[seq_spec, st_spec, st_spec],
        out_shape=[jax.ShapeDtypeStruct((nb, t_len, d), BF16),
                   jax.ShapeDtypeStruct(x0_re.shape, F32), jax.ShapeDtypeStruct(x0_re.shape, F32)],
        scratch_shapes=[pltpu.VMEM((uc, 2 * sp), BF16), pltpu.VMEM((2 * sp, uc), BF16),
                        pltpu.VMEM((1, sp), F32), pltpu.VMEM((1, sp), F32),
                        pltpu.VMEM((2 * sp // LANES * nb // 8, steps * 8, LANES), F32),
                        pltpu.VMEM((8, sp * nb // 8), F32), pltpu.VMEM((8, sp * nb // 8), F32)],
        compiler_params=_params(56, ("parallel", "arbitrary")),
        name="s5_scan",
    )(u.reshape(nb, t_len, d), lre, lim, ldt, bre, bim, cre, cim, d_skip.reshape(1, d), x0_re, x0_im)
    return y.reshape(rows_total, d), f_re, f_im


def _row_tile(m, cap):
    best = None
    for t in range(16, cap + 1, 16):
        if m % t == 0:
            best = t
    return best if best is not None else m


def _col_tile(n, cap):
    best = LANES
    for t in range(LANES, cap + 1, LANES):
        if n % t == 0:
            best = t
    return best


def kernel(x_prompt, x_sample, cache_k, cache_v, cache_logf, state_s5_re, state_s5_im, page_table,
           meta_tokens, norm_g, ffn_w_up, ffn_w_down, attn_w_in, attn_b_f, attn_w_out,
           s5_A_re, s5_A_im, s5_B_re, s5_B_im, s5_C_re, s5_C_im, s5_log_dt, s5_D, s5_w_glu):
    bsz, seq, d = x_prompt.shape
    db, ds, _ = x_sample.shape
    n_meta = meta_tokens.shape[0]
    t_len = n_meta + seq
    mp, ms = bsz * t_len, db * ds
    depth = norm_g.shape[0]
    n_heads = attn_b_f.shape[-1]
    n_groups, state_dim = s5_A_re.shape[1:]

    tm = _row_tile(t_len, 704)
    tn_qkv = _col_tile(3 * d, 2048)
    tn_out = _col_tile(d, 1024)
    tn_glu = _col_tile(d, 512)
    tq = 256 if t_len >= 256 else LANES
    n_pages = page_table.shape[1]
    pages_per_step = max(p for p in (8, 4, 2, 1) if n_pages % p == 0)
    n_attn = (depth + 1) // 2

    w_in = attn_w_in.astype(BF16)
    w_out = attn_w_out.astype(BF16)
    w_glu = s5_w_glu.astype(BF16)

    meta = jnp.broadcast_to(meta_tokens[None].astype(x_prompt.dtype), (bsz, n_meta, d))
    hp = jnp.concatenate([meta, x_prompt], axis=1).reshape(mp, d)
    hs = x_sample.reshape(ms, d)

    lf_pr, sre_pr, sim_pr = [], [], []
    k_sa, v_sa, lf_sa, sre_sa, sim_sa = [], [], [], [], []
    kv_t = None
    up = _rmsnorm(hp, norm_g[0, 0], tm)
    us = _rmsnorm(hs, norm_g[0, 0], ms)
    for i in range(depth):
        g = norm_g[i]
        attn_layer = i % 2 == 0
        act_p, act_s, w_down = _ffn_up(up, us, ffn_w_up, ffn_w_down, i, 0, tm)
        hp, up, hs, us = _ffn_down(act_p, act_s, w_down, hp, hs, g[1], g[2],
                                   BF16 if attn_layer else F32, tm)
        if attn_layer:
            li = i // 2
            qkv_p, logf_p = _qkv_proj(up, w_in, attn_b_f, li, tm, tn_qkv)
            qkv_s, logf_s = _qkv_proj(us, w_in, attn_b_f, li, ms, tn_qkv)
            cparts = _prompt_cumsum(logf_p, bsz, t_len)
            yp, kv_t = _attn_prompt(qkv_p, cparts, bsz, t_len, n_heads, tq, li, n_attn, kv_t)
            ys = _attn_decode(qkv_s, logf_s, cache_k, cache_v, cache_logf, page_table, li,
                              pages_per_step)
            hp, up, hs, us = _mixer_out(yp, ys, w_out, li, False, hp, hs, g[3], g[4], tm, tn_out)
            lf_pr.append(logf_p)
            k_sa.append(qkv_s[:, d:2 * d]); v_sa.append(qkv_s[:, 2 * d:]); lf_sa.append(logf_s)
        else:
            si = i // 2
            layout = _s5_layout(s5_A_re[si], s5_A_im[si], s5_B_re[si], s5_B_im[si],
                                s5_C_re[si], s5_C_im[si], s5_log_dt[si], 16)
            zeros = jnp.zeros((8, n_groups * state_dim), F32)
            yp, re_p, im_p = _s5_scan(up, layout, s5_D[si], zeros, zeros, bsz, tm)
            ys, re_s, im_s = _s5_scan(us, layout, s5_D[si],
                                      state_s5_re[si].reshape(db, -1), state_s5_im[si].reshape(db, -1),
                                      db, ds)
            hp, up, hs, us = _mixer_out(yp, ys, w_glu, si, True, hp, hs, g[3], g[4], tm, tn_glu)
            sre_pr.append(re_p[:bsz]); sim_pr.append(im_p[:bsz])
            sre_sa.append(re_s[:db]); sim_sa.append(im_s[:db])
        act_p, act_s, w_down = _ffn_up(up, us, ffn_w_up, ffn_w_down, i, 1, tm)
        g_next = norm_g[i + 1, 0] if i + 1 < depth else None
        hp, up, hs, us = _ffn_down(act_p, act_s, w_down, hp, hs, g[5], g_next, BF16, tm)

    hd = d // n_heads
    stack = lambda xs, shape: jnp.stack(xs).reshape((len(xs),) + shape)
    kv_out = lambda x: x.reshape(n_attn, bsz, n_heads, hd, t_len).transpose(0, 1, 4, 2, 3)
    return (hp.reshape(bsz, t_len, d)[:, n_meta:], hs.reshape(db, ds, d),
            kv_out(kv_t[0]), kv_out(kv_t[1]),
            stack(lf_pr, (bsz, t_len, n_heads)),
            stack(sre_pr, (bsz, n_groups, state_dim)), stack(sim_pr, (bsz, n_groups, state_dim)),
            stack(k_sa, (db, ds, n_heads, hd)), stack(v_sa, (db, ds, n_heads, hd)),
            stack(lf_sa, (db, ds, n_heads)),
            stack(sre_sa, (db, n_groups, state_dim)), stack(sim_sa, (db, n_groups, state_dim)))
```

```python
import functools

import jax
import jax.numpy as jnp
from jax import lax
from jax.experimental import pallas as pl
from jax.experimental.pallas import tpu as pltpu

F32 = jnp.float32
BF16 = jnp.bfloat16
RMS_EPS = 1e-6
LOG2E = 1.4426950408889634
LANES = 128
MIB = 1024 * 1024
NT_DIMS = (((1,), (1,)), ((), ()))


def _params(vmem_mib, semantics=None):
    return pltpu.CompilerParams(dimension_semantics=semantics, vmem_limit_bytes=vmem_mib * MIB)


def _rms(x, g):
    ms = jnp.mean(x * x, axis=-1, keepdims=True)
    return (x * lax.rsqrt(ms + RMS_EPS)) * g


def _split3(x):
    hi = x.astype(BF16)
    r1 = x - hi.astype(F32)
    mid = r1.astype(BF16)
    lo = (r1 - mid.astype(F32)).astype(BF16)
    return hi, mid, lo


def _dot(a, b):
    return jnp.dot(a, b, preferred_element_type=F32)


def _dot_nt(a, b):
    return lax.dot_general(a, b, NT_DIMS, preferred_element_type=F32)


def _rmsnorm_kernel(h_ref, g_ref, o_ref):
    o_ref[...] = _rms(h_ref[...], g_ref[...]).astype(o_ref.dtype)


def _rmsnorm(h, g, tm):
    m, d = h.shape
    return pl.pallas_call(
        _rmsnorm_kernel,
        grid=(m // tm,),
        in_specs=[pl.BlockSpec((tm, d), lambda i: (i, 0)), pl.BlockSpec((1, d), lambda i: (0, 0))],
        out_specs=pl.BlockSpec((tm, d), lambda i: (i, 0)),
        out_shape=jax.ShapeDtypeStruct((m, d), BF16),
        compiler_params=_params(40),
        name="rmsnorm",
    )(h, g.reshape(1, d))


def _ffn_up_kernel(up_ref, us_ref, wa_ref, wb_ref, wd_ref, op_ref, os_ref, wdb_ref, w_scr,
                   *, n_chunks, ch):
    w_scr[:, :LANES] = wa_ref[...].astype(BF16)
    w_scr[:, LANES:] = wb_ref[...].astype(BF16)
    wdb_ref[...] = wd_ref[...].astype(BF16)

    def swiglu(x):
        z = _dot(x, w_scr[...])
        a = z[:, :LANES]
        b = z[:, LANES:]
        return (a * jax.nn.sigmoid(a) * b).astype(BF16)

    def body(c, carry):
        rows = pl.ds(pl.multiple_of(c * ch, ch), ch)
        op_ref[rows, :] = swiglu(up_ref[rows, :])
        return carry

    lax.fori_loop(0, n_chunks, body, 0, unroll=True)
    os_ref[...] = swiglu(us_ref[...])


def _ffn_up(up, us, w_up, w_down, layer, half, ch):
    mp, d = up.shape
    ms = us.shape[0]
    f = w_up.shape[-1] // 2
    nt = f // LANES
    kern = functools.partial(_ffn_up_kernel, n_chunks=mp // ch, ch=ch)
    return pl.pallas_call(
        kern,
        grid=(nt,),
        in_specs=[
            pl.BlockSpec(memory_space=pltpu.VMEM),
            pl.BlockSpec(memory_space=pltpu.VMEM),
            pl.BlockSpec((None, None, d, LANES), lambda j: (layer, half, 0, j)),
            pl.BlockSpec((None, None, d, LANES), lambda j: (layer, half, 0, j + nt)),
            pl.BlockSpec((None, None, LANES, d), lambda j: (layer, half, j, 0)),
        ],
        out_specs=[pl.BlockSpec((mp, LANES), lambda j: (0, j)),
                   pl.BlockSpec((ms, LANES), lambda j: (0, j)),
                   pl.BlockSpec((LANES, d), lambda j: (j, 0))],
        out_shape=[jax.ShapeDtypeStruct((mp, f), BF16), jax.ShapeDtypeStruct((ms, f), BF16),
                   jax.ShapeDtypeStruct((f, d), BF16)],
        scratch_shapes=[pltpu.VMEM((d, 2 * LANES), BF16)],
        compiler_params=_params(58),
        name="ffn_up",
    )(up, us, w_up, w_up, w_down)


def _residual_epilogue(y, h_ref, gp_ref, gn_ref, hn_ref, un_ref, scale):
    hn = h_ref[...] + scale * _rms(y, gp_ref[...])
    hn_ref[...] = hn
    if un_ref is not None:
        un_ref[...] = _rms(hn, gn_ref[...]).astype(un_ref.dtype)


def _down_tiling(f):
    nt = f // LANES
    for dmul in (8, 6, 4, 2, 7, 5, 3):
        if nt % dmul == 0 and nt // dmul >= 1:
            return dmul * LANES, nt // dmul, False
    for dmul in (8, 6, 4, 2, 7, 5, 3):
        if (nt - 1) % dmul == 0:
            return dmul * LANES, (nt - 1) // dmul, True
    return LANES, nt, False


class _RowRefs:
    def __init__(self, x, xt, h, hn, un, acc):
        self.x, self.xt, self.h, self.hn, self.un, self.acc = x, xt, h, hn, un, acc


def _unpack_rows(refs, has_tail, has_next):
    refs = list(refs)
    take = lambda cond=True: refs.pop(0) if cond else None
    xp, xtp, xs, xts = take(), take(has_tail), take(), take(has_tail)
    return refs, xp, xtp, xs, xts


def _ffn_down_kernel(*refs, nk, scale, has_tail, has_next):
    refs, xp, xtp, xs, xts = _unpack_rows(refs, has_tail, has_next)
    take = lambda cond=True: refs.pop(0) if cond else None
    wm_ref, wt_ref = take(), take(has_tail)
    hp, hs, gp_ref, gn_ref = take(), take(), take(), take(has_next)
    hnp, unp, hns, uns = take(), take(has_next), take(), take(has_next)
    accp, accs = take(), take()
    prompt = _RowRefs(xp, xtp, hp, hnp, unp, accp)
    sample = _RowRefs(xs, xts, hs, hns, uns, accs)
    i = pl.program_id(0)
    k = pl.program_id(1)

    def accumulate(r):
        @pl.when(k == 0)
        def _():
            if has_tail:
                r.acc[...] = _dot(r.xt[...], wt_ref[...])
            else:
                r.acc[...] = jnp.zeros(r.acc.shape, F32)

        r.acc[...] += _dot(r.x[...], wm_ref[...])

        @pl.when(k == nk - 1)
        def _():
            _residual_epilogue(r.acc[...], r.h, gp_ref, gn_ref, r.hn, r.un, scale)

    accumulate(prompt)
    pl.when(i == 0)(lambda: accumulate(sample))


def _ffn_down(act_p, act_s, w_down, h_p, h_s, g_post, g_next, u_dtype, tm):
    mp, f = act_p.shape
    ms = act_s.shape[0]
    d = h_p.shape[1]
    tk, nk, has_tail = _down_tiling(f)
    has_next = g_next is not None
    tail_blk = (f - LANES) // LANES
    in_specs = [pl.BlockSpec((tm, tk), lambda i, k: (i, k))]
    args = [act_p]
    if has_tail:
        in_specs.append(pl.BlockSpec((tm, LANES), lambda i, k: (i, tail_blk)))
        args.append(act_p)
    in_specs.append(pl.BlockSpec((ms, tk), lambda i, k: (0, k)))
    args.append(act_s)
    if has_tail:
        in_specs.append(pl.BlockSpec((ms, LANES), lambda i, k: (0, tail_blk)))
        args.append(act_s)
    in_specs.append(pl.BlockSpec((tk, d), lambda i, k: (k, 0)))
    args.append(w_down)
    if has_tail:
        in_specs.append(pl.BlockSpec((LANES, d), lambda i, k: (tail_blk, 0)))
        args.append(w_down)
    row_spec = pl.BlockSpec((tm, d), lambda i, k: (i, 0))
    srow_spec = pl.BlockSpec((ms, d), lambda i, k: (0, 0))
    vec_spec = pl.BlockSpec((1, d), lambda i, k: (0, 0))
    in_specs += [row_spec, srow_spec, vec_spec]
    args += [h_p, h_s, g_post.reshape(1, d)]
    out_specs = [row_spec]
    out_shape = [jax.ShapeDtypeStruct((mp, d), F32)]
    if has_next:
        in_specs.append(vec_spec)
        args.append(g_next.reshape(1, d))
        out_specs.append(row_spec)
        out_shape.append(jax.ShapeDtypeStruct((mp, d), u_dtype))
    out_specs.append(srow_spec)
    out_shape.append(jax.ShapeDtypeStruct((ms, d), F32))
    if has_next:
        out_specs.append(srow_spec)
        out_shape.append(jax.ShapeDtypeStruct((ms, d), u_dtype))
    kern = functools.partial(_ffn_down_kernel, nk=nk, scale=0.5, has_tail=has_tail, has_next=has_next)
    res = pl.pallas_call(
        kern,
        grid=(mp // tm, nk),
        in_specs=in_specs,
        out_specs=out_specs,
        out_shape=out_shape,
        scratch_shapes=[pltpu.VMEM((tm, d), F32), pltpu.VMEM((ms, d), F32)],
        compiler_params=_params(56, ("arbitrary", "arbitrary")),
        name="ffn_down",
    )(*args)
    if has_next:
        return res[0], res[1], res[2], res[3]
    return res[0], None, res[1], None


def _mixer_out_kernel(xp, xs, *refs, nn, tn, glu):
    refs = list(refs)
    take = lambda cond=True: refs.pop(0) if cond else None
    wa_ref, wb_ref = take(), take(glu)
    hp, hs, gp_ref, gn_ref = take(), take(), take(), take()
    hnp, unp, hns, uns = take(), take(), take(), take()
    accp, accs = take(), take()
    prompt = _RowRefs(xp, None, hp, hnp, unp, accp)
    sample = _RowRefs(xs, None, hs, hns, uns, accs)
    i = pl.program_id(0)
    j = pl.program_id(1)
    col = pl.multiple_of(j * tn, tn)

    def project(r):
        x = r.x[...].astype(BF16)
        if glu:
            y = _dot(x, wa_ref[...]) * jax.nn.sigmoid(_dot(x, wb_ref[...]))
        else:
            y = _dot(x, wa_ref[...])
        r.acc[:, pl.ds(col, tn)] = y

        @pl.when(j == nn - 1)
        def _():
            _residual_epilogue(r.acc[...], r.h, gp_ref, gn_ref, r.hn, r.un, 1.0)

    project(prompt)
    pl.when(i == 0)(lambda: project(sample))


def _mixer_out(x_p, x_s, w, layer, glu, h_p, h_s, g_post, g_next, tm, tn):
    mp, d = h_p.shape
    ms = h_s.shape[0]
    kdim = x_s.shape[1]
    nn = d // tn
    row_spec = pl.BlockSpec((tm, d), lambda i, j: (i, 0))
    srow_spec = pl.BlockSpec((ms, d), lambda i, j: (0, 0))
    vec_spec = pl.BlockSpec((1, d), lambda i, j: (0, 0))
    in_specs = [pl.BlockSpec((tm, kdim), lambda i, j: (i, 0)),
                pl.BlockSpec((ms, kdim), lambda i, j: (0, 0)),
                pl.BlockSpec((None, kdim, tn), lambda i, j: (layer, 0, j))]
    args = [x_p, x_s, w]
    if glu:
        in_specs.append(pl.BlockSpec((None, kdim, tn), lambda i, j: (layer, 0, j + nn)))
        args.append(w)
    in_specs += [row_spec, srow_spec, vec_spec, vec_spec]
    args += [h_p, h_s, g_post.reshape(1, d), g_next.reshape(1, d)]
    return pl.pallas_call(
        functools.partial(_mixer_out_kernel, nn=nn, tn=tn, glu=glu),
        grid=(mp // tm, nn),
        in_specs=in_specs,
        out_specs=[row_spec, row_spec, srow_spec, srow_spec],
        out_shape=[jax.ShapeDtypeStruct((mp, d), F32), jax.ShapeDtypeStruct((mp, d), BF16),
                   jax.ShapeDtypeStruct((ms, d), F32), jax.ShapeDtypeStruct((ms, d), BF16)],
        scratch_shapes=[pltpu.VMEM((tm, d), F32), pltpu.VMEM((ms, d), F32)],
        compiler_params=_params(56, ("arbitrary", "arbitrary")),
        name="mixer_out_glu" if glu else "mixer_out",
    )(*args)


def _log_sigmoid(x):
    return jnp.minimum(x, 0.0) - jnp.log1p(jnp.exp(-jnp.abs(x)))


def _qkv_kernel(x_ref, w_ref, wf_ref, bf_ref, o_ref, lf_ref, *, n_heads):
    j = pl.program_id(1)
    x = x_ref[...]
    o_ref[...] = _dot(x, w_ref[...])

    @pl.when(j == 0)
    def _():
        f = _dot(x, wf_ref[...])[:, :n_heads] + bf_ref[...]
        lf_ref[...] = _log_sigmoid(f)


def _qkv_proj(u, w_in, b_f, layer, tm, tn):
    m, d = u.shape
    n_heads = b_f.shape[-1]
    n_main = 3 * d
    return pl.pallas_call(
        functools.partial(_qkv_kernel, n_heads=n_heads),
        grid=(m // tm, n_main // tn),
        in_specs=[
            pl.BlockSpec((tm, d), lambda i, j: (i, 0)),
            pl.BlockSpec((None, d, tn), lambda i, j: (layer, 0, j)),
            pl.BlockSpec((None, d, LANES), lambda i, j: (layer, 0, n_main // LANES)),
            pl.BlockSpec((None, 1, n_heads), lambda i, j: (layer, 0, 0)),
        ],
        out_specs=[pl.BlockSpec((tm, tn), lambda i, j: (i, j)),
                   pl.BlockSpec((tm, n_heads), lambda i, j: (i, 0))],
        out_shape=[jax.ShapeDtypeStruct((m, n_main), F32), jax.ShapeDtypeStruct((m, n_heads), F32)],
        compiler_params=_params(48, ("parallel", "arbitrary")),
        name="qkv_proj",
    )(u, w_in, w_in, b_f.reshape(b_f.shape[0], 1, n_heads))


def _cumsum_kernel(lf_ref, o_ref, *, t_len, chunk, n_heads):
    r = lax.broadcasted_iota(jnp.int32, (chunk, chunk), 0)
    c = lax.broadcasted_iota(jnp.int32, (chunk, chunk), 1)
    tri = jnp.where(c <= r, 1.0, 0.0).astype(BF16)
    carry = jnp.zeros((1, n_heads), F32)
    for s in range(t_len // chunk):
        x = lf_ref[s * chunk:(s + 1) * chunk, :]
        hi, mid, lo = _split3(x)
        cs = _dot(tri, hi) + _dot(tri, mid) + _dot(tri, lo) + carry
        carry = cs[chunk - 1:chunk, :]
        chi, cmid, clo = _split3(cs * LOG2E)
        o_ref[s * chunk:(s + 1) * chunk, 0:n_heads] = chi
        o_ref[s * chunk:(s + 1) * chunk, n_heads:2 * n_heads] = cmid
        o_ref[s * chunk:(s + 1) * chunk, 2 * n_heads:3 * n_heads] = clo


def _cumsum_chunk(t_len):
    for c in (688, 256, 128, 64, 48, 16):
        if t_len % c == 0:
            return c
    return t_len


def _prompt_cumsum(logf, batch, t_len):
    n_heads = logf.shape[1]
    kern = functools.partial(_cumsum_kernel, t_len=t_len, chunk=_cumsum_chunk(t_len), n_heads=n_heads)
    return pl.pallas_call(
        kern,
        grid=(batch,),
        in_specs=[pl.BlockSpec((t_len, n_heads), lambda b: (b, 0))],
        out_specs=pl.BlockSpec((t_len, 3 * n_heads), lambda b: (b, 0)),
        out_shape=jax.ShapeDtypeStruct((batch * t_len, 3 * n_heads), BF16),
        compiler_params=_params(32, ("parallel",)),
        name="prompt_cumsum",
    )(logf)


def _store_transposed(src_ref, dst_ref, t_len):
    n_full = t_len // LANES
    for c in range(n_full):
        dst_ref[:, c * LANES:(c + 1) * LANES] = src_ref[c * LANES:(c + 1) * LANES, :].T
    rem = t_len - n_full * LANES
    if rem:
        last = src_ref[t_len - LANES:t_len, :].T
        dst_ref[:, n_full * LANES:t_len] = last[:, LANES - rem:]


def _attn_prompt_kernel(q_ref, k_ref, v_ref, cp_ref, *rest, t_len, tq, n_heads, head_dim, aliased):
    if aliased:
        rest = rest[2:]
    o_ref, kt_ref, vt_ref, qa_scr, ka_scr, v_scr = rest
    _store_transposed(k_ref, kt_ref, t_len)
    _store_transposed(v_ref, vt_ref, t_len)
    hp = pl.program_id(1)
    half = LANES // 2
    lane = lax.broadcasted_iota(jnp.int32, (1, LANES), 1)
    r3 = lax.broadcasted_iota(jnp.int32, (3 * n_heads, LANES), 0)
    l3 = lax.broadcasted_iota(jnp.int32, (3 * n_heads, LANES), 1)
    cp = cp_ref[...]
    q = (q_ref[...] * (head_dim ** -0.5 * LOG2E)).astype(BF16)
    k = k_ref[...].astype(BF16)
    v_scr[...] = v_ref[...].astype(BF16)
    for hh in range(2):
        head = 2 * hp + hh
        base = half * (1 - hh)
        sel_q = jnp.zeros((3 * n_heads, LANES), F32)
        sel_k = jnp.zeros((3 * n_heads, LANES), F32)
        for part in range(3):
            row_hit = r3 == head + part * n_heads
            sel_q = sel_q + jnp.where(row_hit & (l3 == base + part), 1.0, 0.0)
            sel_k = sel_k + jnp.where(row_hit & (l3 == base + 3 + part), -1.0, 0.0)
        ones_q = jnp.where((lane >= base + 3) & (lane < base + 6), 1.0, 0.0)
        ones_k = jnp.where((lane >= base) & (lane < base + 3), 1.0, 0.0)
        ex_q = (_dot(cp, sel_q.astype(BF16)) + ones_q).astype(BF16)
        ex_k = (_dot(cp, sel_k.astype(BF16)) + ones_k).astype(BF16)
        data = (lane >= half * hh) & (lane < half * hh + half)
        qa_scr[hh] = jnp.where(data, q, ex_q)
        ka_scr[hh] = jnp.where(data, k, ex_k)

    n_tiles = t_len // tq
    tiles = [(i * tq, tq) for i in range(n_tiles - 1)] + [((n_tiles - 1) * tq, tq + t_len % tq)]
    units = [(q0, tl, hh) for q0, tl in tiles for hh in range(2)]

    def scores(q0, tl, hh):
        qt = qa_scr[hh, q0:q0 + tl, :]
        row = lax.broadcasted_iota(jnp.int32, (tl, tl), 0)
        col = lax.broadcasted_iota(jnp.int32, (tl, tl), 1)
        sd = jnp.where(col <= row, _dot_nt(qt, ka_scr[hh, q0:q0 + tl, :]), -jnp.inf)
        so = _dot_nt(qt, ka_scr[hh, 0:q0, :]) if q0 > 0 else None
        return sd, so

    def softmax(sd, so):
        m = jnp.max(sd, axis=-1, keepdims=True)
        if so is not None:
            m = jnp.maximum(m, jnp.max(so, axis=-1, keepdims=True))
        pd = jnp.exp2(sd - m)
        den = jnp.sum(pd, axis=-1, keepdims=True)
        po = None
        if so is not None:
            po = jnp.exp2(so - m)
            den = den + jnp.sum(po, axis=-1, keepdims=True)
            po = po.astype(BF16)
        return pd.astype(BF16), po, den

    def values(q0, tl, pd, po, den):
        o = _dot(pd, v_scr[q0:q0 + tl, :])
        if po is not None:
            o = o + _dot(po, v_scr[0:q0, :])
        return o / den

    staged_s, staged_p, done = {}, {}, {}
    for k in range(len(units) + 2):
        if k < len(units):
            staged_s[k] = scores(*units[k])
        if 0 <= k - 1 < len(units):
            staged_p[k - 1] = softmax(*staged_s.pop(k - 1))
        if 0 <= k - 2 < len(units):
            q0, tl, hh = units[k - 2]
            done[hh] = values(q0, tl, *staged_p.pop(k - 2))
            if hh == 1:
                o_ref[q0:q0 + tl, :] = jnp.where(lane < half, done[0], done[1]).astype(o_ref.dtype)


def _attn_prompt(qkv, cparts, batch, t_len, n_heads, tq, layer, n_layers, kv_t):
    d = qkv.shape[1] // 3
    head_dim = d // n_heads
    assert 2 * head_dim == LANES, "head pairs must fill one lane tile"
    assert t_len % 16 == 0 and t_len >= LANES
    npair = n_heads // 2
    aliased = kv_t is not None
    kern = functools.partial(_attn_prompt_kernel, t_len=t_len, tq=tq,
                             n_heads=n_heads, head_dim=head_dim, aliased=aliased)
    blk = lambda off: pl.BlockSpec((t_len, LANES), lambda b, p: (b, off + p))
    in_specs = [blk(0), blk(npair), blk(2 * npair),
                pl.BlockSpec((t_len, 3 * n_heads), lambda b, p: (b, 0))]
    args = [qkv, qkv, qkv, cparts]
    aliases = {}
    if aliased:
        in_specs += [pl.BlockSpec(memory_space=pl.ANY), pl.BlockSpec(memory_space=pl.ANY)]
        args += list(kv_t)
        aliases = {4: 1, 5: 2}
    t_spec = pl.BlockSpec((None, None, LANES, t_len), lambda b, p: (layer, b, p, 0))
    t_shape = jax.ShapeDtypeStruct((n_layers, batch, d, t_len), F32)
    y, k_t, v_t = pl.pallas_call(
        kern,
        grid=(batch, npair),
        in_specs=in_specs,
        out_specs=[pl.BlockSpec((t_len, LANES), lambda b, p: (b, p)), t_spec, t_spec],
        out_shape=[jax.ShapeDtypeStruct((batch * t_len, d), BF16), t_shape, t_shape],
        scratch_shapes=[pltpu.VMEM((2, t_len, LANES), BF16), pltpu.VMEM((2, t_len, LANES), BF16),
                        pltpu.VMEM((t_len, LANES), BF16)],
        input_output_aliases=aliases,
        compiler_params=_params(48, ("parallel", "arbitrary")),
        name="attn_prompt",
    )(*args)
    return y, (k_t, v_t)


def _attn_decode_kernel(pt_ref, q_ref, kn_ref, vn_ref, lfn_ref, *rest,
                        n_pg, n_heads, head_dim, n_new, page, n_steps):
    del pt_ref
    k_refs = rest[:n_pg]
    v_refs = rest[n_pg:2 * n_pg]
    lf_refs = rest[2 * n_pg:3 * n_pg]
    o_ref = rest[3 * n_pg]
    (q_scr, s_scr, p_scr, acc_scr, m_scr, l_scr, cnew_scr, alpha_scr,
     tail_scr, b_scr) = rest[3 * n_pg + 1:]
    step = pl.program_id(1)
    rows = n_heads * n_new
    head_unroll = 8 if n_heads % 8 == 0 else 1
    er = lax.broadcasted_iota(jnp.int32, (rows, n_heads), 0)
    ec = lax.broadcasted_iota(jnp.int32, (rows, n_heads), 1)
    expand = jnp.where((er >= ec * n_new) & (er < (ec + 1) * n_new), 1.0, 0.0).astype(BF16)

    def spread(lf):
        return [_dot_nt(expand, part).astype(BF16) for part in _split3(lf)]

    def head_update(h8, p_h, v_h, first):
        contrib = _dot(p_h, v_h)
        if first:
            acc_scr[h8, :] = contrib
        else:
            acc_scr[h8, :] = alpha_scr[h8, 0:head_dim] * acc_scr[h8, :] + contrib

    @pl.when(step == 0)
    def _():
        scale = head_dim ** -0.5
        for h in range(n_heads):
            q_scr[h * n_new:(h + 1) * n_new, :] = q_ref[:, h * head_dim:(h + 1) * head_dim] * scale
        kr = lax.broadcasted_iota(jnp.int32, (n_new, LANES), 0)
        kc = lax.broadcasted_iota(jnp.int32, (n_new, LANES), 1)
        upper = jnp.where(kr <= kc, 1.0, 0.0).astype(BF16)
        parts = spread(lfn_ref[...])
        cum = _dot(parts[0], upper) + _dot(parts[1], upper) + _dot(parts[2], upper)
        rq = lax.broadcasted_iota(jnp.int32, (rows, LANES), 0)
        lq = lax.broadcasted_iota(jnp.int32, (rows, LANES), 1)
        qpos = rq & (n_new - 1)
        c_q = jnp.sum(jnp.where(lq == qpos, cum, 0.0), axis=-1, keepdims=True)
        cnew_scr[...] = jnp.broadcast_to(c_q, (rows, LANES))
        bias = c_q - cum
        valid = lq <= qpos
        s_scr[:, 0:LANES] = jnp.zeros((rows, LANES), F32)
        for h in range(n_heads):
            h8 = pl.ds(h * n_new, n_new)
            k_h = kn_ref[:, h * head_dim:(h + 1) * head_dim].astype(BF16)
            s_scr[h8, 0:n_new] = _dot_nt(q_scr[h8, :].astype(BF16), k_h)
        s = jnp.where(valid, s_scr[:, 0:LANES] + bias, -jnp.inf)
        m = jnp.max(s, axis=-1, keepdims=True)
        p = jnp.exp(s - m)
        m_scr[...] = jnp.broadcast_to(m, (rows, LANES))
        l_scr[...] = p
        tail_scr[...] = jnp.zeros((n_heads, LANES), F32)
        p_scr[:, 0:LANES] = p
        for h in range(n_heads):
            h8 = pl.ds(h * n_new, n_new)
            v_h = vn_ref[:, h * head_dim:(h + 1) * head_dim].astype(BF16)
            head_update(h8, p_scr[h8, 0:n_new].astype(BF16), v_h, True)

    kr = lax.broadcasted_iota(jnp.int32, (page, 2 * page), 0)
    kc = lax.broadcasted_iota(jnp.int32, (page, 2 * page), 1)
    after = jnp.where((kr > kc) | (kc >= page), 1.0, 0.0).astype(BF16)

    tail = tail_scr[...]
    for j in range(n_pg):
        hi, mid, lo = _split3(lf_refs[j][...])
        full = _dot(hi, after) + _dot(mid, after) + _dot(lo, after)
        b_scr[:, j * page:(j + 1) * page] = full[:, 0:page] + tail
        tail = tail + full[:, page:2 * page]
    tail_scr[...] = tail

    def scores(h, carry):
        h8 = pl.ds(pl.multiple_of(h * n_new, n_new), n_new)
        q_h = q_scr[h8, :].astype(BF16)
        k_h = jnp.concatenate([k_refs[j][h] for j in range(n_pg)], axis=1).astype(BF16)
        s_scr[h8, :] = _dot(q_h, k_h) + b_scr[pl.ds(h, 1), :]
        return carry

    lax.fori_loop(0, n_heads, scores, 0, unroll=head_unroll)

    cnew = cnew_scr[...]
    s_all = [s_scr[:, j * page:(j + 1) * page] + cnew for j in range(n_pg)]
    m_chunk = s_all[0]
    for j in range(1, n_pg):
        m_chunk = jnp.maximum(m_chunk, s_all[j])
    m_old = m_scr[...]
    m_new = jnp.maximum(m_old, jnp.max(m_chunk, axis=-1, keepdims=True))
    alpha = jnp.exp(m_old - m_new)
    m_scr[...] = m_new
    alpha_scr[...] = alpha
    l_new = alpha * l_scr[...]
    for j in range(n_pg):
        p = jnp.exp(s_all[j] - m_new)
        l_new = l_new + p
        p_scr[:, j * page:(j + 1) * page] = p
    l_scr[...] = l_new

    def values(h, carry):
        h8 = pl.ds(pl.multiple_of(h * n_new, n_new), n_new)
        v_h = jnp.concatenate([v_refs[j][h] for j in range(n_pg)], axis=1).astype(BF16)
        contrib = _dot_nt(p_scr[h8, :].astype(BF16), v_h)
        acc_scr[h8, :] = alpha_scr[h8, 0:head_dim] * acc_scr[h8, :] + contrib
        return carry

    lax.fori_loop(0, n_heads, values, 0, unroll=head_unroll)

    @pl.when(step == n_steps - 1)
    def _():
        den = jnp.sum(l_scr[...], axis=-1, keepdims=True)
        res = acc_scr[...] / den
        for h in range(n_heads):
            o_ref[:, h * head_dim:(h + 1) * head_dim] = res[h * n_new:(h + 1) * n_new, :]


def _attn_decode(qkv_s, logf_s, cache_k, cache_v, cache_logf, page_table, layer, n_pg):
    n_layers, n_pool, page, n_heads, head_dim = cache_k.shape
    db, n_pages = page_table.shape
    n_new = qkv_s.shape[0] // db
    d = n_heads * head_dim
    rows = n_heads * n_new
    n_steps = n_pages // n_pg
    ck = cache_k.transpose(0, 1, 3, 4, 2)
    cv = cache_v.transpose(0, 1, 3, 4, 2)
    clf = cache_logf.transpose(0, 1, 3, 2)

    def page_of(b, s, pt, j):
        return pt[b * n_pages + (n_pages - 1 - (s * n_pg + j))]

    kv_spec = lambda j: pl.BlockSpec((None, None, n_heads, head_dim, page),
                                     lambda b, s, pt: (layer, page_of(b, s, pt, j), 0, 0, 0))
    lf_spec = lambda j: pl.BlockSpec((None, None, n_heads, page),
                                     lambda b, s, pt: (layer, page_of(b, s, pt, j), 0, 0))
    new_spec = lambda c: pl.BlockSpec((n_new, d), lambda b, s, pt: (b, c))
    in_specs = [new_spec(0), new_spec(1), new_spec(2),
                pl.BlockSpec((n_new, n_heads), lambda b, s, pt: (b, 0))]
    in_specs += [kv_spec(j) for j in range(n_pg)] + [kv_spec(j) for j in range(n_pg)]
    in_specs += [lf_spec(j) for j in range(n_pg)]
    kern = functools.partial(_attn_decode_kernel, n_pg=n_pg, n_heads=n_heads, head_dim=head_dim,
                             n_new=n_new, page=page, n_steps=n_steps)
    stat = pltpu.VMEM((rows, LANES), F32)
    grid_spec = pltpu.PrefetchScalarGridSpec(
        num_scalar_prefetch=1,
        grid=(db, n_steps),
        in_specs=in_specs,
        out_specs=pl.BlockSpec((n_new, d), lambda b, s, pt: (b, 0)),
        scratch_shapes=[pltpu.VMEM((rows, head_dim), F32),
                        pltpu.VMEM((rows, n_pg * page), F32),
                        pltpu.VMEM((rows, n_pg * page), F32),
                        pltpu.VMEM((rows, head_dim), F32),
                        stat, stat, stat, stat,
                        pltpu.VMEM((n_heads, LANES), F32),
                        pltpu.VMEM((n_heads, n_pg * page), F32)],
    )
    return pl.pallas_call(
        kern,
        grid_spec=grid_spec,
        out_shape=jax.ShapeDtypeStruct((db * n_new, d), F32),
        compiler_params=_params(56, ("parallel", "arbitrary")),
        name="attn_decode",
    )(page_table.reshape(-1), qkv_s, qkv_s, qkv_s, logf_s,
      *([ck] * n_pg), *([cv] * n_pg), *([clf] * n_pg))


def _s5_kernel(u_ref, lre_ref, lim_ref, ldt_ref, bre_ref, bim_ref, cre_ref, cim_ref, d_ref,
               x0r_ref, x0i_ref, y_ref, fr_ref, fi_ref,
               w_scr, c_scr, ar_scr, ai_scr, xs_scr, sr_scr, si_scr, *, nb, n_tc, sp):
    tc = pl.program_id(1)
    steps = xs_scr.shape[1] // 8
    nct = sp // LANES
    fold = 8 // nb
    ppp = nct // fold
    wf = ppp * LANES

    def seq_planes(b):
        for part in range(2):
            for j in range(nct):
                half, jj = divmod(j, ppp)
                yield (part * nct + j, part * ppp + jj, pl.ds(half * nb + b, steps, stride=8))

    def store_seq(b, val):
        for col, plane, rows_b in seq_planes(b):
            xs_scr[plane, rows_b, :] = val[:, col * LANES:(col + 1) * LANES]

    def load_seq(b):
        return jnp.concatenate([xs_scr[plane, rows_b, :] for _, plane, rows_b in seq_planes(b)], axis=1)

    def fold_tiles(x):
        if fold == 1:
            return x
        low = lax.broadcasted_iota(jnp.int32, (8, wf), 0) < nb
        return jnp.where(low, x[:, :wf], pltpu.roll(x[:, wf:], nb, 0))

    def unfold_tiles(x):
        if fold == 1:
            return x
        return jnp.concatenate([x, pltpu.roll(x, nb, 0)], axis=1)

    @pl.when(tc == 0)
    def _():
        lre = lre_ref[...]
        lim = lim_ref[...]
        dt = jnp.exp(ldt_ref[...])
        mag = jnp.exp(lre * dt)
        ar = mag * jnp.cos(lim * dt)
        ai = mag * jnp.sin(lim * dt)
        ar_scr[...] = ar
        ai_scr[...] = ai
        xr = ar - 1.0
        den = lre * lre + lim * lim
        cr = (xr * lre + ai * lim) / den
        ci = (ai * lre - xr * lim) / den
        uc, p_dim = bre_ref.shape
        c_dim = cre_ref.shape[1]

        def block_diag(x_ref, out_rows, out_cols, row_blk, col_blk):
            tr = lax.broadcasted_iota(jnp.int32, (col_blk, out_cols), 0)
            tcol = lax.broadcasted_iota(jnp.int32, (col_blk, out_cols), 1)
            rep = jnp.where((tcol & (col_blk - 1)) == tr, 1.0, 0.0).astype(BF16)
            hi, mid, lo = _split3(x_ref[...])
            full = _dot(hi, rep) + _dot(mid, rep) + _dot(lo, rep)
            rr = lax.broadcasted_iota(jnp.int32, (out_rows, out_cols), 0)
            cc = lax.broadcasted_iota(jnp.int32, (out_rows, out_cols), 1)
            same = (rr >> (row_blk.bit_length() - 1)) == (cc >> (col_blk.bit_length() - 1))
            return jnp.where(same, full, 0.0)

        bre = block_diag(bre_ref, uc, sp, c_dim, p_dim)
        bim = block_diag(bim_ref, uc, sp, c_dim, p_dim)
        w_scr[:, 0:sp] = (bre * cr - bim * ci).astype(BF16)
        w_scr[:, sp:2 * sp] = (bre * ci + bim * cr).astype(BF16)
        c_scr[0:sp, :] = block_diag(cre_ref, sp, uc, p_dim, c_dim).astype(BF16)
        c_scr[sp:2 * sp, :] = (-block_diag(cim_ref, sp, uc, p_dim, c_dim)).astype(BF16)
        sr_scr[...] = fold_tiles(x0r_ref[...])
        si_scr[...] = fold_tiles(x0i_ref[...])

    for b in range(nb):
        store_seq(b, _dot(u_ref[b].astype(BF16), w_scr[...]))
    arf = fold_tiles(jnp.broadcast_to(ar_scr[...], (8, sp)))
    aif = fold_tiles(jnp.broadcast_to(ai_scr[...], (8, sp)))

    def body(s, carry):
        pr, pi = carry
        r0 = pl.ds(pl.multiple_of(s * 8, 8), 8)
        br = jnp.concatenate([xs_scr[jj, r0, :] for jj in range(ppp)], axis=1)
        bi = jnp.concatenate([xs_scr[ppp + jj, r0, :] for jj in range(ppp)], axis=1)
        xr = arf * pr - aif * pi + br
        xi = arf * pi + aif * pr + bi
        for jj in range(ppp):
            xs_scr[jj, r0, :] = xr[:, jj * LANES:(jj + 1) * LANES]
            xs_scr[ppp + jj, r0, :] = xi[:, jj * LANES:(jj + 1) * LANES]
        return xr, xi

    fr, fi = lax.fori_loop(0, steps, body, (sr_scr[...], si_scr[...]), unroll=8)
    sr_scr[...] = fr
    si_scr[...] = fi
    for b in range(nb):
        y = _dot(load_seq(b).astype(BF16), c_scr[...]) + d_ref[...] * u_ref[b]
        y_ref[b] = jax.nn.gelu(y).astype(y_ref.dtype)

    @pl.when(tc == n_tc - 1)
    def _():
        fr_ref[...] = unfold_tiles(fr)
        fi_ref[...] = unfold_tiles(fi)


def _s5_layout(a_re, a_im, b_re, b_im, c_re, c_im, log_dt, gc):
    g, p = a_re.shape
    c = b_re.shape[-1]
    nch = g // gc
    assert p & (p - 1) == 0 and c & (c - 1) == 0, "block masks use shifts"
    row = lambda x: x.reshape(nch, 1, gc * p)
    b_rows = lambda b: b.transpose(0, 2, 1).reshape(nch, gc * c, p)
    c_rows = lambda cm: cm.transpose(0, 2, 1).reshape(nch, gc * p, c)
    return (row(a_re), row(a_im), row(jnp.repeat(log_dt, p)),
            b_rows(b_re), b_rows(b_im), c_rows(c_re), c_rows(c_im))


def _s5_scan(u, layout, d_skip, x0_re, x0_im, nb, steps):
    lre, lim, ldt, bre, bim, cre, cim = layout
    rows_total, d = u.shape
    nch, uc, p_dim = bre.shape
    sp, c_dim = cre.shape[1:]
    t_len = rows_total // nb
    n_tc = t_len // steps
    assert t_len % steps == 0 and (steps * nb) % 8 == 0 and nb in (4, 8)
    kern = functools.partial(_s5_kernel, nb=nb, n_tc=n_tc, sp=sp)
    par = lambda r, c: pl.BlockSpec((None, r, c), lambda g, t: (g, 0, 0))
    st_spec = pl.BlockSpec((8, sp), lambda g, t: (0, g))
    seq_spec = pl.BlockSpec((nb, steps, uc), lambda g, t: (0, t, g))
    y, f_re, f_im = pl.pallas_call(
        kern,
        grid=(nch, n_tc),
        in_specs=[seq_spec, par(1, sp), par(1, sp), par(1, sp),
                  par(uc, p_dim), par(uc, p_dim), par(sp, c_dim), par(sp, c_dim),
                  pl.BlockSpec((1, uc), lambda g, t: (0, g)),
                  st_spec, st_spec],
        out_specs=[seq_spec, st_spec, st_spec],
        out_shape=[jax.ShapeDtypeStruct((nb, t_len, d), BF16),
                   jax.ShapeDtypeStruct(x0_re.shape, F32), jax.ShapeDtypeStruct(x0_re.shape, F32)],
        scratch_shapes=[pltpu.VMEM((uc, 2 * sp), BF16), pltpu.VMEM((2 * sp, uc), BF16),
                        pltpu.VMEM((1, sp), F32), pltpu.VMEM((1, sp), F32),
                        pltpu.VMEM((2 * sp // LANES * nb // 8, steps * 8, LANES), F32),
                        pltpu.VMEM((8, sp * nb // 8), F32), pltpu.VMEM((8, sp * nb // 8), F32)],
        compiler_params=_params(56, ("parallel", "arbitrary")),
        name="s5_scan",
    )(u.reshape(nb, t_len, d), lre, lim, ldt, bre, bim, cre, cim, d_skip.reshape(1, d), x0_re, x0_im)
    return y.reshape(rows_total, d), f_re, f_im


def _row_tile(m, cap):
    best = None
    for t in range(16, cap + 1, 16):
        if m % t == 0:
            best = t
    return best if best is not None else m


def _col_tile(n, cap):
    best = LANES
    for t in range(LANES, cap + 1, LANES):
        if n % t == 0:
            best = t
    return best


def kernel(x_prompt, x_sample, cache_k, cache_v, cache_logf, state_s5_re, state_s5_im, page_table,
           meta_tokens, norm_g, ffn_w_up, ffn_w_down, attn_w_in, attn_b_f, attn_w_out,
           s5_A_re, s5_A_im, s5_B_re, s5_B_im, s5_C_re, s5_C_im, s5_log_dt, s5_D, s5_w_glu):
    bsz, seq, d = x_prompt.shape
    db, ds, _ = x_sample.shape
    n_meta = meta_tokens.shape[0]
    t_len = n_meta + seq
    mp, ms = bsz * t_len, db * ds
    depth = norm_g.shape[0]
    n_heads = attn_b_f.shape[-1]
    n_groups, state_dim = s5_A_re.shape[1:]

    tm = _row_tile(t_len, 704)
    tn_qkv = _col_tile(3 * d, 2048)
    tn_out = _col_tile(d, 1024)
    tn_glu = _col_tile(d, 512)
    tq = 256 if t_len >= 256 else LANES
    n_pages = page_table.shape[1]
    pages_per_step = max(p for p in (8, 4, 2, 1) if n_pages % p == 0)
    n_attn = (depth + 1) // 2

    w_in = attn_w_in.astype(BF16)
    w_out = attn_w_out.astype(BF16)
    w_glu = s5_w_glu.astype(BF16)

    meta = jnp.broadcast_to(meta_tokens[None].astype(x_prompt.dtype), (bsz, n_meta, d))
    hp = jnp.concatenate([meta, x_prompt], axis=1).reshape(mp, d)
    hs = x_sample.reshape(ms, d)

    lf_pr, sre_pr, sim_pr = [], [], []
    k_sa, v_sa, lf_sa, sre_sa, sim_sa = [], [], [], [], []
    kv_t = None
    up = _rmsnorm(hp, norm_g[0, 0], tm)
    us = _rmsnorm(hs, norm_g[0, 0], ms)
    for i in range(depth):
        g = norm_g[i]
        attn_layer = i % 2 == 0
        act_p, act_s, w_down = _ffn_up(up, us, ffn_w_up, ffn_w_down, i, 0, tm)
        hp, up, hs, us = _ffn_down(act_p, act_s, w_down, hp, hs, g[1], g[2],
                                   BF16 if attn_layer else F32, tm)
        if attn_layer:
            li = i // 2
            qkv_p, logf_p = _qkv_proj(up, w_in, attn_b_f, li, tm, tn_qkv)
            qkv_s, logf_s = _qkv_proj(us, w_in, attn_b_f, li, ms, tn_qkv)
            cparts = _prompt_cumsum(logf_p, bsz, t_len)
            yp, kv_t = _attn_prompt(qkv_p, cparts, bsz, t_len, n_heads, tq, li, n_attn, kv_t)
            ys = _attn_decode(qkv_s, logf_s, cache_k, cache_v, cache_logf, page_table, li,
                              pages_per_step)
            hp, up, hs, us = _mixer_out(yp, ys, w_out, li, False, hp, hs, g[3], g[4], tm, tn_out)
            lf_pr.append(logf_p)
            k_sa.append(qkv_s[:, d:2 * d]); v_sa.append(qkv_s[:, 2 * d:]); lf_sa.append(logf_s)
        else:
            si = i // 2
            layout = _s5_layout(s5_A_re[si], s5_A_im[si], s5_B_re[si], s5_B_im[si],
                                s5_C_re[si], s5_C_im[si], s5_log_dt[si], 16)
            zeros = jnp.zeros((8, n_groups * state_dim), F32)
            yp, re_p, im_p = _s5_scan(up, layout, s5_D[si], zeros, zeros, bsz, tm)
            ys, re_s, im_s = _s5_scan(us, layout, s5_D[si],
                                      state_s5_re[si].reshape(db, -1), state_s5_im[si].reshape(db, -1),
                                      db, ds)
            hp, up, hs, us = _mixer_out(yp, ys, w_glu, si, True, hp, hs, g[3], g[4], tm, tn_glu)
            sre_pr.append(re_p[:bsz]); sim_pr.append(im_p[:bsz])
            sre_sa.append(re_s[:db]); sim_sa.append(im_s[:db])
        act_p, act_s, w_down = _ffn_up(up, us, ffn_w_up, ffn_w_down, i, 1, tm)
        g_next = norm_g[i + 1, 0] if i + 1 < depth else None
        hp, up, hs, us = _ffn_down(act_p, act_s, w_down, hp, hs, g[5], g_next, BF16, tm)

    hd = d // n_heads
    stack = lambda xs, shape: jnp.stack(xs).reshape((len(xs),) + shape)
    kv_out = lambda x: x.reshape(n_attn, bsz, n_heads, hd, t_len).transpose(0, 1, 4, 2, 3)
    return (hp.reshape(bsz, t_len, d)[:, n_meta:], hs.reshape(db, ds, d),
            kv_out(kv_t[0]), kv_out(kv_t[1]),
            stack(lf_pr, (bsz, t_len, n_heads)),
            stack(sre_pr, (bsz, n_groups, state_dim)), stack(sim_pr, (bsz, n_groups, state_dim)),
            stack(k_sa, (db, ds, n_heads, hd)), stack(v_sa, (db, ds, n_heads, hd)),
            stack(lf_sa, (db, ds, n_heads)),
            stack(sre_sa, (db, n_groups, state_dim)), stack(sim_sa, (db, n_groups, state_dim)))
```
